```python
import math
import jax
import jax.numpy as jnp
from jax import lax
import numpy as np

D_MODEL = 2048
BATCH = 1
SEQ = 16384
DEPTH = 4
DEC_BATCH = 4
DEC_SEQ = 2048
PAST_LEN = 128

MIX_WIDTH = D_MODEL
GROUP_WIDTH = MIX_WIDTH // 4

SSD_WIDTH = GROUP_WIDTH
SSD_HEAD_DIM = 64
SSD_HEADS = SSD_WIDTH // SSD_HEAD_DIM
SSD_GROUPS = 2
SSD_STATE = 128
SSD_CHUNK = 128
SSD_CONV_DIM = SSD_WIDTH + 2 * SSD_GROUPS * SSD_STATE

GDN_WIDTH = GROUP_WIDTH
GDN_HEAD_DIM = 128
GDN_HEADS = GDN_WIDTH // GDN_HEAD_DIM
GDN_CHUNK = 64
GDN_CONV_DIM = 3 * GDN_WIDTH

CONV_K = 5

FNET_WIDTH = GROUP_WIDTH
FNET_HEADS = 4
FNET_HEAD_DIM = FNET_WIDTH // FNET_HEADS

S5_WIDTH = GROUP_WIDTH
S5_GROUP_CH = 16
S5_GROUPS = S5_WIDTH // S5_GROUP_CH
S5_STATE = 64

SSD_IN = 2 * SSD_WIDTH + 2 * SSD_GROUPS * SSD_STATE + 2 * SSD_HEADS
GDN_IN = 4 * GDN_WIDTH + 4 * GDN_HEADS
IN_COLS = SSD_IN + GDN_IN + FNET_WIDTH + S5_WIDTH

D_FF = 5632
N_EXPERTS = 8
TOP_K = 2
EXPERT_FF = 7168
N_DENSE = (DEPTH + 1) // 2
N_MOE = DEPTH // 2

EPS = 1e-6
DT_MIN = 1e-3
DT_MAX = 1e-1

kernel_name = 'hybrid_bidir_ssd_gdn_fnet_s5_encoder'


def rmsnorm(x, w):
    xf = x.astype(jnp.float32)
    y = xf * lax.rsqrt(jnp.mean(jnp.square(xf), axis=-1, keepdims=True) + EPS)
    return (y * w.astype(jnp.float32)).astype(x.dtype)


def l2norm(t):
    return t * lax.rsqrt(jnp.sum(jnp.square(t), axis=-1, keepdims=True) + EPS)


def flip(t):
    return jnp.flip(t, axis=1)


def centred_dwconv(x, w):
    rhs = jnp.transpose(w)[:, None, :].astype(x.dtype)
    return lax.conv_general_dilated(x, rhs, window_strides=(1,), padding=[(CONV_K // 2, CONV_K // 2)],
                                    dimension_numbers=('NWC', 'WIO', 'NWC'), feature_group_count=x.shape[-1])


def swiglu(t, wg, wu, wd):
    return (jax.nn.silu(t @ wg) * (t @ wu)) @ wd


def ssd_scan(x, dt, a, bm, cm):
    Bn, L, H, P = x.shape
    N = bm.shape[-1]
    Q = SSD_CHUNK
    c = L // Q
    xdt = (x * dt[..., None]).reshape(Bn, c, Q, H, P)
    bm = bm.reshape(Bn, c, Q, H, N)
    cm = cm.reshape(Bn, c, Q, H, N)
    acum = jnp.cumsum(jnp.moveaxis((dt * a).reshape(Bn, c, Q, H), -1, 2), axis=-1)
    causal = jnp.tril(jnp.ones((Q, Q), dtype=bool))
    seg = jnp.exp(jnp.where(causal, acum[..., :, None] - acum[..., None, :], -jnp.inf))
    scores = jnp.einsum('bclhn,bcshn->bchls', cm, bm) * seg
    y_diag = jnp.einsum('bchls,bcshp->bclhp', scores, xdt)
    decay_to_end = jnp.exp(acum[..., -1:] - acum)
    chunk_states = jnp.einsum('bclhn,bchl,bclhp->bchpn', bm, decay_to_end, xdt)
    chunk_decay = jnp.exp(acum[..., -1])

    def step(h, inp):
        s, d = inp
        return h * d[..., None, None] + s, h

    h0 = jnp.zeros((Bn, H, P, N), x.dtype)
    _, prev = lax.scan(step, h0, (jnp.moveaxis(chunk_states, 1, 0), jnp.moveaxis(chunk_decay, 1, 0)))
    prev = jnp.moveaxis(prev, 0, 1)
    y_off = jnp.einsum('bclhn,bchpn,bchl->bclhp', cm, prev, jnp.exp(acum))
    return (y_diag + y_off).reshape(Bn, L, H, P)


def ssd_mixer(u, conv_w, conv_b, dt_bias, a_log, d_skip, norm_w):
    f32 = jnp.float32
    Bn, L, _ = u.shape
    z = u[..., :SSD_WIDTH]
    xbc = u[..., SSD_WIDTH:SSD_WIDTH + SSD_CONV_DIM]
    dt_raw = u[..., SSD_WIDTH + SSD_CONV_DIM:]
    xbc = jax.nn.silu(centred_dwconv(xbc, conv_w.astype(f32)) + conv_b.astype(f32))
    x = xbc[..., :SSD_WIDTH].reshape(Bn, L, SSD_HEADS, SSD_HEAD_DIM)
    rep = SSD_HEADS // SSD_GROUPS
    bm = jnp.repeat(xbc[..., SSD_WIDTH:SSD_WIDTH + SSD_GROUPS * SSD_STATE].reshape(Bn, L, SSD_GROUPS, SSD_STATE), rep, axis=2)
    cm = jnp.repeat(xbc[..., SSD_WIDTH + SSD_GROUPS * SSD_STATE:].reshape(Bn, L, SSD_GROUPS, SSD_STATE), rep, axis=2)
    dt = jax.nn.softplus(dt_raw.reshape(Bn, L, 2, SSD_HEADS) + dt_bias.astype(f32))
    a = -jnp.exp(a_log.astype(f32))
    y_f = ssd_scan(x, dt[:, :, 0], a[0], bm, cm)
    y_b = flip(ssd_scan(flip(x), flip(dt[:, :, 1]), a[1], flip(bm), flip(cm)))
    y = (y_f + y_b + x * d_skip.astype(f32)[:, None]).reshape(Bn, L, SSD_WIDTH)
    return rmsnorm(y * jax.nn.silu(z), norm_w)


def gated_delta_chunked(q, k, v, g, beta):
    Bn, L, H, K = q.shape
    V = v.shape[-1]
    Q = GDN_CHUNK
    c = L // Q

    def chunks(t):
        return jnp.swapaxes(t.reshape(Bn, c, Q, H, *t.shape[3:]), 2, 3)

    q, k, v, g, beta = chunks(q), chunks(k), chunks(v), chunks(g), chunks(beta)
    gc = jnp.cumsum(g, axis=-1)
    causal = jnp.tril(jnp.ones((Q, Q), dtype=bool))
    strict = jnp.tril(jnp.ones((Q, Q), dtype=bool), -1)
    decay = jnp.exp(jnp.where(causal, gc[..., :, None] - gc[..., None, :], -jnp.inf))
    kb = k * beta[..., None]
    lower = jnp.where(strict, jnp.einsum('bchik,bchjk->bchij', kb, k) * decay, 0.0)
    eye = jnp.eye(Q, dtype=q.dtype)
    rhs = jnp.concatenate([v * beta[..., None], kb * jnp.exp(gc)[..., None]], axis=-1)
    sol = lax.linalg.triangular_solve(eye + lower, rhs, left_side=True, lower=True, unit_diagonal=True)
    u_c, w_c = sol[..., :V], sol[..., V:]
    intra = jnp.einsum('bchik,bchjk->bchij', q, k) * decay

    def step(S, inp):
        q_i, k_i, u_i, w_i, gc_i, a_i = inp
        v_new = u_i - jnp.einsum('bhqk,bhkv->bhqv', w_i, S)
        o = jnp.einsum('bhqk,bhkv->bhqv', q_i * jnp.exp(gc_i)[..., None], S) + jnp.einsum('bhij,bhjv->bhiv', a_i, v_new)
        g_last = gc_i[..., -1]
        S = S * jnp.exp(g_last)[..., None, None] + jnp.einsum(
            'bhqk,bhqv->bhkv', k_i * jnp.exp(g_last[..., None] - gc_i)[..., None], v_new)
        return S, o

    xs = tuple(jnp.moveaxis(t, 1, 0) for t in (q, k, u_c, w_c, gc, intra))
    S0 = jnp.zeros((Bn, H, K, V), q.dtype)
    _, o = lax.scan(step, S0, xs)
    return jnp.transpose(o, (1, 0, 3, 2, 4)).reshape(Bn, L, H, V)


def gdn_mixer(u, conv_w, dt_bias, a_log, norm_w):
    f32 = jnp.float32
    Bn, L, _ = u.shape
    qkv = jax.nn.silu(centred_dwconv(u[..., :GDN_CONV_DIM], conv_w.astype(f32)))
    z = u[..., GDN_CONV_DIM:4 * GDN_WIDTH].reshape(Bn, L, GDN_HEADS, GDN_HEAD_DIM)
    a_raw = u[..., 4 * GDN_WIDTH:4 * GDN_WIDTH + 2 * GDN_HEADS].reshape(Bn, L, 2, GDN_HEADS)
    b_raw = u[..., 4 * GDN_WIDTH + 2 * GDN_HEADS:].reshape(Bn, L, 2, GDN_HEADS)
    q = l2norm(qkv[..., :GDN_WIDTH].reshape(Bn, L, GDN_HEADS, GDN_HEAD_DIM)) * (GDN_HEAD_DIM ** -0.5)
    k = l2norm(qkv[..., GDN_WIDTH:2 * GDN_WIDTH].reshape(Bn, L, GDN_HEADS, GDN_HEAD_DIM))
    v = qkv[..., 2 * GDN_WIDTH:].reshape(Bn, L, GDN_HEADS, GDN_HEAD_DIM)
    g = -jnp.exp(a_log.astype(f32)) * jax.nn.softplus(a_raw + dt_bias.astype(f32))
    beta = jax.nn.sigmoid(b_raw)
    o_f = gated_delta_chunked(q, k, v, g[:, :, 0], beta[:, :, 0])
    o_b = flip(gated_delta_chunked(flip(q), flip(k), flip(v), flip(g[:, :, 1]), flip(beta[:, :, 1])))
    o = rmsnorm(o_f + o_b, norm_w) * jax.nn.silu(z)
    return o.reshape(Bn, L, GDN_WIDTH)


def fourier_mixer(u, w, b):
    Bn, L, _ = u.shape
    uh = u.reshape(Bn, L, FNET_HEADS, FNET_HEAD_DIM).astype(jnp.float32)
    f = jnp.fft.fft2(uh, axes=(1, 3), norm='ortho').real
    return f.reshape(Bn, L, FNET_WIDTH) @ w.astype(jnp.float32) + b.astype(jnp.float32)


def _lin_rec(e1, e2):
    a1, b1 = e1
    a2, b2 = e2
    return a1 * a2, a2 * b1 + b2


def s5_mixer(u, lam_re, lam_im, log_step, b_re, b_im, c_re, c_im, d_skip, glu_w, glu_b):
    f32 = jnp.float32
    Bn, L, _ = u.shape
    ug = u.reshape(Bn, L, S5_GROUPS, S5_GROUP_CH)
    bu = lax.complex(jnp.einsum('blgc,gpc->blgp', ug, b_re.astype(f32)),
                     jnp.einsum('blgc,gpc->blgp', ug, b_im.astype(f32)))
    lam = lax.complex(lam_re.astype(f32), lam_im.astype(f32))
    delta = jnp.exp(log_step.astype(f32))[..., None]
    lam_bar = jnp.exp(lam * delta)
    b_scale = (lam_bar - 1.0) / lam
    c = lax.complex(c_re.astype(f32), c_im.astype(f32))
    y = u * d_skip.astype(f32)
    for direction, reverse in ((0, False), (1, True)):
        a = jnp.broadcast_to(lam_bar[direction], (1, L, S5_GROUPS, S5_STATE))
        _, states = lax.associative_scan(_lin_rec, (a, bu * b_scale[direction]), reverse=reverse, axis=1)
        y = y + jnp.einsum('blgp,gcp->blgc', states, c[direction]).real.reshape(Bn, L, S5_WIDTH)
    h = jax.nn.gelu(y)
    return h * jax.nn.sigmoid(h @ glu_w.astype(f32) + glu_b.astype(f32))


def moe_ffn(h, router_w, w_gate, w_up, w_down):
    shape = h.shape
    t = h.reshape(-1, shape[-1])
    logits = (t @ router_w).astype(jnp.float32)
    top_logit, top_idx = lax.top_k(logits, TOP_K)
    top_w = jax.nn.softmax(top_logit, axis=-1)
    combine = jnp.einsum('tk,tke->te', top_w, jax.nn.one_hot(top_idx, N_EXPERTS, dtype=jnp.float32))
    out = jnp.zeros(t.shape, jnp.float32)
    for e in range(N_EXPERTS):
        out = out + combine[:, e:e + 1] * swiglu(t, w_gate[e], w_up[e], w_down[e]).astype(jnp.float32)
    return out.astype(h.dtype).reshape(shape)


def mixer_block(x, p, i):
    h = rmsnorm(x, p['norm_mix'][i])
    u = (h @ p['w_in'][i]).astype(jnp.float32)
    o1 = SSD_IN
    o2 = o1 + GDN_IN
    o3 = o2 + FNET_WIDTH
    y_ssd = ssd_mixer(u[..., :o1], p['ssd_conv_w'][i], p['ssd_conv_b'][i], p['ssd_dt_bias'][i],
                      p['ssd_a_log'][i], p['ssd_d'][i], p['ssd_norm'][i])
    y_gdn = gdn_mixer(u[..., o1:o2], p['gdn_conv_w'][i], p['gdn_dt_bias'][i], p['gdn_a_log'][i], p['gdn_norm'][i])
    y_fnet = fourier_mixer(u[..., o2:o3], p['fnet_w'][i], p['fnet_b'][i])
    y_s5 = s5_mixer(u[..., o3:], p['s5_lambda_re'][i], p['s5_lambda_im'][i], p['s5_log_step'][i],
                    p['s5_b_re'][i], p['s5_b_im'][i], p['s5_c_re'][i], p['s5_c_im'][i], p['s5_d'][i],
                    p['s5_glu_w'][i], p['s5_glu_b'][i])
    y = jnp.concatenate([y_ssd, y_gdn, y_fnet, y_s5], axis=-1).astype(x.dtype)
    return x + y @ p['w_out'][i]


def channel_block(x, p, i):
    h = rmsnorm(x, p['norm_ffn'][i])
    j = i // 2
    if i % 2 == 0:
        return x + swiglu(h, p['ffn_w_gate'][j], p['ffn_w_up'][j], p['ffn_w_down'][j])
    return x + moe_ffn(h, p['router_w'][j], p['moe_w_gate'][j], p['moe_w_up'][j], p['moe_w_down'][j])


def trunk(x, p):
    for i in range(DEPTH):
        x = mixer_block(x, p, i)
        x = channel_block(x, p, i)
    return rmsnorm(x, p['norm_final'])


def setup_inputs(seed: int = 0) -> dict:
    f32 = jnp.float32
    key = jax.random.key(seed)
    counter = [0]

    def nk():
        counter[0] += 1
        return jax.random.fold_in(key, counter[0])

    def normal(shape, scale):
        return jax.random.normal(nk(), shape, f32) * scale

    def gain(shape):
        return 1.0 + 0.02 * jax.random.normal(nk(), shape, f32)

    def log_uniform(shape, lo, hi):
        return jax.random.uniform(nk(), shape, f32, math.log(lo), math.log(hi))

    def dt_bias(shape):
        dt = jnp.exp(log_uniform(shape, DT_MIN, DT_MAX))
        return dt + jnp.log(-jnp.expm1(-dt))

    def a_log(shape):
        return jnp.log(jax.random.uniform(nk(), shape, f32, 1.0, 16.0))

    s5_shape = (DEPTH, 2, S5_GROUPS, S5_STATE)
    return {
        'x_prompt': normal((BATCH, SEQ, D_MODEL), 1.0),
        'x_sample': normal((DEC_BATCH, DEC_SEQ, D_MODEL), 1.0),
        'norm_mix': gain((DEPTH, D_MODEL)),
        'w_in': normal((DEPTH, D_MODEL, IN_COLS), D_MODEL ** -0.5),
        'ssd_conv_w': normal((DEPTH, SSD_CONV_DIM, CONV_K), CONV_K ** -0.5),
        'ssd_conv_b': normal((DEPTH, SSD_CONV_DIM), 0.02),
        'ssd_dt_bias': dt_bias((DEPTH, 2, SSD_HEADS)),
        'ssd_a_log': a_log((DEPTH, 2, SSD_HEADS)),
        'ssd_d': gain((DEPTH, SSD_HEADS)),
        'ssd_norm': gain((DEPTH, SSD_WIDTH)),
        'gdn_conv_w': normal((DEPTH, GDN_CONV_DIM, CONV_K), CONV_K ** -0.5),
        'gdn_dt_bias': dt_bias((DEPTH, 2, GDN_HEADS)),
        'gdn_a_log': a_log((DEPTH, 2, GDN_HEADS)),
        'gdn_norm': gain((DEPTH, GDN_HEAD_DIM)),
        'fnet_w': normal((DEPTH, FNET_WIDTH, FNET_WIDTH), FNET_WIDTH ** -0.5),
        'fnet_b': normal((DEPTH, FNET_WIDTH), 0.02),
        's5_lambda_re': -0.5 + normal(s5_shape, 0.01),
        's5_lambda_im': jnp.pi * jnp.arange(S5_STATE, dtype=f32) + normal(s5_shape, 0.01),
        's5_log_step': log_uniform((DEPTH, 2, S5_GROUPS), DT_MIN, DT_MAX),
        's5_b_re': normal((DEPTH, S5_GROUPS, S5_STATE, S5_GROUP_CH), (2 * S5_GROUP_CH) ** -0.5),
        's5_b_im': normal((DEPTH, S5_GROUPS, S5_STATE, S5_GROUP_CH), (2 * S5_GROUP_CH) ** -0.5),
        's5_c_re': normal((DEPTH, 2, S5_GROUPS, S5_GROUP_CH, S5_STATE), S5_STATE ** -0.5),
        's5_c_im': normal((DEPTH, 2, S5_GROUPS, S5_GROUP_CH, S5_STATE), S5_STATE ** -0.5),
        's5_d': normal((DEPTH, S5_WIDTH), 0.5),
        's5_glu_w': normal((DEPTH, S5_WIDTH, S5_WIDTH), S5_WIDTH ** -0.5),
        's5_glu_b': normal((DEPTH, S5_WIDTH), 0.02),
        'w_out': normal((DEPTH, MIX_WIDTH, D_MODEL), MIX_WIDTH ** -0.5),
        'norm_ffn': gain((DEPTH, D_MODEL)),
        'ffn_w_gate': normal((N_DENSE, D_MODEL, D_FF), D_MODEL ** -0.5),
        'ffn_w_up': normal((N_DENSE, D_MODEL, D_FF), D_MODEL ** -0.5),
        'ffn_w_down': normal((N_DENSE, D_FF, D_MODEL), D_FF ** -0.5),
        'router_w': normal((N_MOE, D_MODEL, N_EXPERTS), D_MODEL ** -0.5),
        'moe_w_gate': normal((N_MOE, N_EXPERTS, D_MODEL, EXPERT_FF), D_MODEL ** -0.5),
        'moe_w_up': normal((N_MOE, N_EXPERTS, D_MODEL, EXPERT_FF), D_MODEL ** -0.5),
        'moe_w_down': normal((N_MOE, N_EXPERTS, EXPERT_FF, D_MODEL), EXPERT_FF ** -0.5),
        'norm_final': gain((D_MODEL,)),
    }


def reference(x_prompt, x_sample, norm_mix, w_in, ssd_conv_w, ssd_conv_b, ssd_dt_bias, ssd_a_log, ssd_d, ssd_norm,
              gdn_conv_w, gdn_dt_bias, gdn_a_log, gdn_norm, fnet_w, fnet_b, s5_lambda_re, s5_lambda_im, s5_log_step,
              s5_b_re, s5_b_im, s5_c_re, s5_c_im, s5_d, s5_glu_w, s5_glu_b, w_out, norm_ffn, ffn_w_gate, ffn_w_up,
              ffn_w_down, router_w, moe_w_gate, moe_w_up, moe_w_down, norm_final):
    p = dict(norm_mix=norm_mix, w_in=w_in, ssd_conv_w=ssd_conv_w, ssd_conv_b=ssd_conv_b, ssd_dt_bias=ssd_dt_bias,
             ssd_a_log=ssd_a_log, ssd_d=ssd_d, ssd_norm=ssd_norm, gdn_conv_w=gdn_conv_w, gdn_dt_bias=gdn_dt_bias,
             gdn_a_log=gdn_a_log, gdn_norm=gdn_norm, fnet_w=fnet_w, fnet_b=fnet_b, s5_lambda_re=s5_lambda_re,
             s5_lambda_im=s5_lambda_im, s5_log_step=s5_log_step, s5_b_re=s5_b_re, s5_b_im=s5_b_im, s5_c_re=s5_c_re,
             s5_c_im=s5_c_im, s5_d=s5_d, s5_glu_w=s5_glu_w, s5_glu_b=s5_glu_b, w_out=w_out, norm_ffn=norm_ffn,
             ffn_w_gate=ffn_w_gate, ffn_w_up=ffn_w_up, ffn_w_down=ffn_w_down, router_w=router_w,
             moe_w_gate=moe_w_gate, moe_w_up=moe_w_up, moe_w_down=moe_w_down, norm_final=norm_final)
    y_prompt = trunk(x_prompt, p)
    y_sample = trunk(x_sample, p)
    return (y_prompt, y_sample)
```

```python
import functools
import math

import jax
import jax.numpy as jnp
from jax import lax
from jax.experimental import pallas as pl
from jax.experimental.pallas import tpu as pltpu

F32 = jnp.float32
BF16 = jnp.bfloat16

D_MODEL = 2048
DEPTH = 4
GROUP_WIDTH = 512

SSD_WIDTH = GROUP_WIDTH
SSD_HEAD_DIM = 64
SSD_HEADS = SSD_WIDTH // SSD_HEAD_DIM
SSD_GROUPS = 2
SSD_STATE = 128
SSD_CHUNK = 128
SSD_CONV_DIM = SSD_WIDTH + 2 * SSD_GROUPS * SSD_STATE

GDN_WIDTH = GROUP_WIDTH
GDN_HEAD_DIM = 128
GDN_HEADS = GDN_WIDTH // GDN_HEAD_DIM
GDN_CHUNK = 64
GDN_CONV_DIM = 3 * GDN_WIDTH

CONV_K = 5

FNET_WIDTH = GROUP_WIDTH
FNET_HEADS = 4
FNET_HEAD_DIM = FNET_WIDTH // FNET_HEADS

S5_WIDTH = GROUP_WIDTH
S5_GROUP_CH = 16
S5_GROUPS = S5_WIDTH // S5_GROUP_CH
S5_STATE = 64

SSD_IN = 2 * SSD_WIDTH + 2 * SSD_GROUPS * SSD_STATE + 2 * SSD_HEADS
GDN_IN = 4 * GDN_WIDTH + 4 * GDN_HEADS
IN_COLS = SSD_IN + GDN_IN + FNET_WIDTH + S5_WIDTH

N_EXPERTS = 8
TOP_K = 2
EPS = 1e-6

LANES = 128
VMEM_LIMIT_BYTES = 56 * 1024 * 1024

U_SSD_Z = 0
U_SSD_XBC = U_SSD_Z + SSD_WIDTH
U_GDN_QKV = U_SSD_XBC + SSD_CONV_DIM
U_GDN_Z = U_GDN_QKV + GDN_CONV_DIM
U_FNET = U_GDN_Z + GDN_WIDTH
U_S5 = U_FNET + FNET_WIDTH
U_WIDE = U_S5 + S5_WIDTH
NARROW = 2 * SSD_HEADS + 4 * GDN_HEADS

ROW_TILE = 1024
FF_TILE = 256


def _params(semantics):
    return pltpu.CompilerParams(dimension_semantics=semantics, vmem_limit_bytes=VMEM_LIMIT_BYTES)


def _rms_normalize(x, gain):
    ms = jnp.mean(x * x, axis=-1, keepdims=True)
    return x * lax.rsqrt(ms + EPS) * gain


def _norm_inproj_kernel(x_ref, g_ref, w_ref, wn_ref, o_ref, on_ref, h_ref):
    @pl.when(pl.program_id(1) == 0)
    def _():
        h = _rms_normalize(x_ref[...], g_ref[...]).astype(BF16)
        h_ref[...] = h
        on_ref[...] = jnp.dot(h, wn_ref[...], preferred_element_type=F32)

    o_ref[...] = jnp.dot(h_ref[...], w_ref[...], preferred_element_type=F32)


def norm_inproj(x, gain, w_wide, w_narrow, *, tm=ROW_TILE, tn=512):
    t, d = x.shape
    n = w_wide.shape[1]
    return pl.pallas_call(
        _norm_inproj_kernel,
        grid=(t // tm, n // tn),
        in_specs=[
            pl.BlockSpec((tm, d), lambda i, j: (i, 0)),
            pl.BlockSpec((1, d), lambda i, j: (0, 0)),
            pl.BlockSpec((d, tn), lambda i, j: (0, j)),
            pl.BlockSpec((d, LANES), lambda i, j: (0, 0)),
        ],
        out_specs=[
            pl.BlockSpec((tm, tn), lambda i, j: (i, j)),
            pl.BlockSpec((tm, LANES), lambda i, j: (i, 0)),
        ],
        out_shape=[jax.ShapeDtypeStruct((t, n), F32), jax.ShapeDtypeStruct((t, LANES), F32)],
        scratch_shapes=[pltpu.VMEM((tm, d), BF16)],
        compiler_params=_params(("parallel", "arbitrary")),
        name="norm_inproj",
    )(x, gain.reshape(1, d), w_wide, w_narrow)


def _outproj_kernel(y_ref, w_ref, x_ref, o_ref):
    o_ref[...] = x_ref[...] + jnp.dot(y_ref[...].astype(BF16), w_ref[...], preferred_element_type=F32)


def outproj_residual(y, w, x, *, tm=512):
    t, d = x.shape
    k = y.shape[1]
    return pl.pallas_call(
        _outproj_kernel,
        grid=(t // tm,),
        in_specs=[
            pl.BlockSpec((tm, k), lambda i: (i, 0)),
            pl.BlockSpec((k, d), lambda i: (0, 0)),
            pl.BlockSpec((tm, d), lambda i: (i, 0)),
        ],
        out_specs=pl.BlockSpec((tm, d), lambda i: (i, 0)),
        out_shape=jax.ShapeDtypeStruct((t, d), F32),
        compiler_params=_params(("parallel",)),
        name="outproj_residual",
    )(y, w, x)


def _swiglu_step(h, wg_ref, wu_ref, wd_ref, acc_ref):
    g = jnp.dot(h, wg_ref[0].astype(BF16), preferred_element_type=F32)
    u = jnp.dot(h, wu_ref[0].astype(BF16), preferred_element_type=F32)
    a = (g * jax.nn.sigmoid(g) * u).astype(BF16)
    acc_ref[...] += jnp.dot(a, wd_ref[0].astype(BF16), preferred_element_type=F32)


def _dense_ffn_kernel(x_ref, g_ref, wg_ref, wu_ref, wd_ref, o_ref, h_ref, acc_ref):
    f = pl.program_id(1)

    @pl.when(f == 0)
    def _():
        h_ref[...] = _rms_normalize(x_ref[...], g_ref[...]).astype(BF16)
        acc_ref[...] = jnp.zeros_like(acc_ref)

    _swiglu_step(h_ref[...], wg_ref, wu_ref, wd_ref, acc_ref)

    @pl.when(f == pl.num_programs(1) - 1)
    def _():
        o_ref[...] = x_ref[...] + acc_ref[...]


def dense_ffn(x, gain, wg, wu, wd, layer, *, tm=512, tf=512):
    t, d = x.shape
    ff = wg.shape[2]
    return pl.pallas_call(
        _dense_ffn_kernel,
        grid=(t // tm, ff // tf),
        in_specs=[
            pl.BlockSpec((tm, d), lambda i, f: (i, 0)),
            pl.BlockSpec((1, d), lambda i, f: (0, 0)),
            pl.BlockSpec((1, d, tf), lambda i, f: (layer, 0, f)),
            pl.BlockSpec((1, d, tf), lambda i, f: (layer, 0, f)),
            pl.BlockSpec((1, tf, d), lambda i, f: (layer, f, 0)),
        ],
        out_specs=pl.BlockSpec((tm, d), lambda i, f: (i, 0)),
        out_shape=jax.ShapeDtypeStruct((t, d), F32),
        scratch_shapes=[pltpu.VMEM((tm, d), BF16), pltpu.VMEM((tm, d), F32)],
        compiler_params=_params(("parallel", "arbitrary")),
        name="dense_ffn",
    )(x, gain.reshape(1, d), wg, wu, wd)


def _moe_ffn_kernel(te_ref, na_ref, x_ref, s_ref, wg_ref, wu_ref, wd_ref, o_ref, acc_ref):
    i = pl.program_id(0)
    f = pl.program_id(1)

    @pl.when(i < na_ref[0])
    def _():
        @pl.when(f == 0)
        def _():
            acc_ref[...] = jnp.zeros_like(acc_ref)

        _swiglu_step(x_ref[...], wg_ref, wu_ref, wd_ref, acc_ref)

        @pl.when(f == pl.num_programs(1) - 1)
        def _():
            o_ref[...] = acc_ref[...] * s_ref[...]


def moe_ffn_grouped(xs, scale, tile_expert, n_active, wg, wu, wd, layer, *, tm=ROW_TILE, tf=FF_TILE):
    p, d = xs.shape
    n_e, ff = wg.shape[1], wg.shape[3]
    n_f = ff // tf
    wg = wg.reshape(-1, d, ff)
    wu = wu.reshape(-1, d, ff)
    wd = wd.reshape(-1, ff, d)

    def row_idx(i, f, te, na):
        return (jnp.minimum(i, na[0] - 1), 0)

    def up_idx(i, f, te, na):
        live = i < na[0]
        return (layer * n_e + te[jnp.minimum(i, na[0] - 1)], 0, jnp.where(live, f, n_f - 1))

    def down_idx(i, f, te, na):
        live = i < na[0]
        return (layer * n_e + te[jnp.minimum(i, na[0] - 1)], jnp.where(live, f, n_f - 1), 0)

    grid_spec = pltpu.PrefetchScalarGridSpec(
        num_scalar_prefetch=2,
        grid=(p // tm, n_f),
        in_specs=[
            pl.BlockSpec((tm, d), row_idx),
            pl.BlockSpec((tm, 1), row_idx),
            pl.BlockSpec((1, d, tf), up_idx),
            pl.BlockSpec((1, d, tf), up_idx),
            pl.BlockSpec((1, tf, d), down_idx),
        ],
        out_specs=pl.BlockSpec((tm, d), row_idx),
        scratch_shapes=[pltpu.VMEM((tm, d), F32)],
    )
    return pl.pallas_call(
        _moe_ffn_kernel,
        grid_spec=grid_spec,
        out_shape=jax.ShapeDtypeStruct((p, d), F32),
        compiler_params=_params(("arbitrary", "arbitrary")),
        name="moe_ffn",
    )(tile_expert, n_active, xs, scale, wg, wu, wd)


def _norm_router_kernel(x_ref, g_ref, wr_ref, h_ref, l_ref):
    h = _rms_normalize(x_ref[...], g_ref[...])
    h_ref[...] = h.astype(BF16)
    l_ref[...] = jnp.dot(h, wr_ref[...], preferred_element_type=F32, precision=lax.Precision.HIGHEST)


def norm_router(x, gain, router_w_padded, *, tm=512):
    t, d = x.shape
    return pl.pallas_call(
        _norm_router_kernel,
        grid=(t // tm,),
        in_specs=[
            pl.BlockSpec((tm, d), lambda i: (i, 0)),
            pl.BlockSpec((1, d), lambda i: (0, 0)),
            pl.BlockSpec((d, LANES), lambda i: (0, 0)),
        ],
        out_specs=[pl.BlockSpec((tm, d), lambda i: (i, 0)), pl.BlockSpec((tm, LANES), lambda i: (i, 0))],
        out_shape=[jax.ShapeDtypeStruct((t, d), BF16), jax.ShapeDtypeStruct((t, LANES), F32)],
        compiler_params=_params(("parallel",)),
        name="norm_router",
    )(x, gain.reshape(1, d), router_w_padded)


def _final_norm_kernel(x_ref, g_ref, o_ref):
    o_ref[...] = _rms_normalize(x_ref[...], g_ref[...])


def final_norm(x, gain, *, tm=512):
    t, d = x.shape
    return pl.pallas_call(
        _final_norm_kernel,
        grid=(t // tm,),
        in_specs=[pl.BlockSpec((tm, d), lambda i: (i, 0)), pl.BlockSpec((1, d), lambda i: (0, 0))],
        out_specs=pl.BlockSpec((tm, d), lambda i: (i, 0)),
        out_shape=jax.ShapeDtypeStruct((t, d), F32),
        compiler_params=_params(("parallel",)),
        name="final_norm",
    )(x, gain.reshape(1, d))


def moe_block(x, gain, router_w, wg, wu, wd, layer, *, tm=ROW_TILE, tf=FF_TILE):
    t, d = x.shape
    n_e = router_w.shape[1]
    wr = jnp.zeros((d, LANES), F32).at[:, :n_e].set(router_w)
    h, logits = norm_router(x, gain, wr)
    top_logit, top_idx = lax.top_k(logits[:, :n_e], TOP_K)
    top_w = jax.nn.softmax(top_logit, axis=-1)

    n_assign = t * TOP_K
    n_tiles = (n_assign + n_e * (tm - 1)) // tm + 1
    flat_e = top_idx.reshape(-1)
    order = jnp.argsort(flat_e, stable=True)
    sorted_e = flat_e[order]
    counts = jnp.zeros((n_e,), jnp.int32).at[flat_e].add(1)
    padded = ((counts + tm - 1) // tm) * tm
    pad_end = jnp.cumsum(padded)
    pad_start = pad_end - padded
    start = jnp.cumsum(counts) - counts
    dest = pad_start[sorted_e] + jnp.arange(n_assign, dtype=jnp.int32) - start[sorted_e]
    src_token = jnp.zeros((n_tiles * tm,), jnp.int32).at[dest].set((order // TOP_K).astype(jnp.int32))
    scale = jnp.zeros((n_tiles * tm,), F32).at[dest].set(top_w.reshape(-1)[order])
    pos = jnp.zeros((n_assign,), jnp.int32).at[order].set(dest).reshape(t, TOP_K)
    tile_start = jnp.arange(n_tiles, dtype=jnp.int32) * tm
    tile_expert = jnp.minimum(jnp.searchsorted(pad_end, tile_start, side="right"), n_e - 1).astype(jnp.int32)
    n_active = (pad_end[-1] // tm).astype(jnp.int32).reshape(1)

    xs = jnp.take(h, src_token, axis=0)
    ys = moe_ffn_grouped(xs, scale.reshape(-1, 1), tile_expert, n_active, wg, wu, wd, layer, tm=tm, tf=tf)
    return x + jnp.take(ys, pos[:, 0], axis=0) + jnp.take(ys, pos[:, 1], axis=0)


def _rmsnorm(x, w):
    return x * lax.rsqrt(jnp.mean(jnp.square(x), axis=-1, keepdims=True) + EPS) * w


def _l2norm(t):
    return t * lax.rsqrt(jnp.sum(jnp.square(t), axis=-1, keepdims=True) + EPS)


def _flip(t):
    return jnp.flip(t, axis=1)


def _centred_dwconv(x, w):
    rhs = jnp.transpose(w)[:, None, :].astype(x.dtype)
    return lax.conv_general_dilated(x, rhs, window_strides=(1,), padding=[(CONV_K // 2, CONV_K // 2)],
                                    dimension_numbers=('NWC', 'WIO', 'NWC'), feature_group_count=x.shape[-1])


def _ssd_scan(x, dt, a, bm, cm):
    Bn, L, H, P = x.shape
    N = bm.shape[-1]
    Q = SSD_CHUNK
    c = L // Q
    xdt = (x * dt[..., None]).reshape(Bn, c, Q, H, P)
    bm = bm.reshape(Bn, c, Q, H, N)
    cm = cm.reshape(Bn, c, Q, H, N)
    acum = jnp.cumsum(jnp.moveaxis((dt * a).reshape(Bn, c, Q, H), -1, 2), axis=-1)
    causal = jnp.tril(jnp.ones((Q, Q), dtype=bool))
    seg = jnp.exp(jnp.where(causal, acum[..., :, None] - acum[..., None, :], -jnp.inf))
    scores = jnp.einsum('bclhn,bcshn->bchls', cm, bm) * seg
    y_diag = jnp.einsum('bchls,bcshp->bclhp', scores, xdt)
    decay_to_end = jnp.exp(acum[..., -1:] - acum)
    chunk_states = jnp.einsum('bclhn,bchl,bclhp->bchpn', bm, decay_to_end, xdt)
    chunk_decay = jnp.exp(acum[..., -1])

    def step(h, inp):
        s, d = inp
        return h * d[..., None, None] + s, h

    h0 = jnp.zeros((Bn, H, P, N), x.dtype)
    _, prev = lax.scan(step, h0, (jnp.moveaxis(chunk_states, 1, 0), jnp.moveaxis(chunk_decay, 1, 0)))
    prev = jnp.moveaxis(prev, 0, 1)
    y_off = jnp.einsum('bclhn,bchpn,bchl->bclhp', cm, prev, jnp.exp(acum))
    return (y_diag + y_off).reshape(Bn, L, H, P)


def ssd_mixer(z, xbc, dt_raw, conv_w, conv_b, dt_bias, a_log, d_skip, norm_w):
    Bn, L, _ = z.shape
    xbc = jax.nn.silu(_centred_dwconv(xbc, conv_w) + conv_b)
    x = xbc[..., :SSD_WIDTH].reshape(Bn, L, SSD_HEADS, SSD_HEAD_DIM)
    rep = SSD_HEADS // SSD_GROUPS
    bm = jnp.repeat(xbc[..., SSD_WIDTH:SSD_WIDTH + SSD_GROUPS * SSD_STATE].reshape(Bn, L, SSD_GROUPS, SSD_STATE), rep, axis=2)
    cm = jnp.repeat(xbc[..., SSD_WIDTH + SSD_GROUPS * SSD_STATE:].reshape(Bn, L, SSD_GROUPS, SSD_STATE), rep, axis=2)
    dt = jax.nn.softplus(dt_raw.reshape(Bn, L, 2, SSD_HEADS) + dt_bias)
    a = -jnp.exp(a_log)
    y_f = _ssd_scan(x, dt[:, :, 0], a[0], bm, cm)
    y_b = _flip(_ssd_scan(_flip(x), _flip(dt[:, :, 1]), a[1], _flip(bm), _flip(cm)))
    y = (y_f + y_b + x * d_skip[:, None]).reshape(Bn, L, SSD_WIDTH)
    return _rmsnorm(y * jax.nn.silu(z), norm_w)


def _gated_delta_chunked(q, k, v, g, beta):
    Bn, L, H, K = q.shape
    V = v.shape[-1]
    Q = GDN_CHUNK
    c = L // Q

    def chunks(t):
        return jnp.swapaxes(t.reshape(Bn, c, Q, H, *t.shape[3:]), 2, 3)

    q, k, v, g, beta = chunks(q), chunks(k), chunks(v), chunks(g), chunks(beta)
    gc = jnp.cumsum(g, axis=-1)
    causal = jnp.tril(jnp.ones((Q, Q), dtype=bool))
    strict = jnp.tril(jnp.ones((Q, Q), dtype=bool), -1)
    decay = jnp.exp(jnp.where(causal, gc[..., :, None] - gc[..., None, :], -jnp.inf))
    kb = k * beta[..., None]
    lower = jnp.where(strict, jnp.einsum('bchik,bchjk->bchij', kb, k) * decay, 0.0)
    eye = jnp.eye(Q, dtype=q.dtype)
    rhs = jnp.concatenate([v * beta[..., None], kb * jnp.exp(gc)[..., None]], axis=-1)
    sol = lax.linalg.triangular_solve(eye + lower, rhs, left_side=True, lower=True, unit_diagonal=True)
    u_c, w_c = sol[..., :V], sol[..., V:]
    intra = jnp.einsum('bchik,bchjk->bchij', q, k) * decay

    def step(S, inp):
        q_i, k_i, u_i, w_i, gc_i, a_i = inp
        v_new = u_i - jnp.einsum('bhqk,bhkv->bhqv', w_i, S)
        o = jnp.einsum('bhqk,bhkv->bhqv', q_i * jnp.exp(gc_i)[..., None], S) + jnp.einsum('bhij,bhjv->bhiv', a_i, v_new)
        g_last = gc_i[..., -1]
        S = S * jnp.exp(g_last)[..., None, None] + jnp.einsum(
            'bhqk,bhqv->bhkv', k_i * jnp.exp(g_last[..., None] - gc_i)[..., None], v_new)
        return S, o

    xs = tuple(jnp.moveaxis(t, 1, 0) for t in (q, k, u_c, w_c, gc, intra))
    S0 = jnp.zeros((Bn, H, K, V), q.dtype)
    _, o = lax.scan(step, S0, xs)
    return jnp.transpose(o, (1, 0, 3, 2, 4)).reshape(Bn, L, H, V)


def gdn_mixer(qkv, z, a_raw, b_raw, conv_w, dt_bias, a_log, norm_w):
    Bn, L, _ = z.shape
    qkv = jax.nn.silu(_centred_dwconv(qkv, conv_w))
    z = z.reshape(Bn, L, GDN_HEADS, GDN_HEAD_DIM)
    a_raw = a_raw.reshape(Bn, L, 2, GDN_HEADS)
    b_raw = b_raw.reshape(Bn, L, 2, GDN_HEADS)
    q = _l2norm(qkv[..., :GDN_WIDTH].reshape(Bn, L, GDN_HEADS, GDN_HEAD_DIM)) * (GDN_HEAD_DIM ** -0.5)
    k = _l2norm(qkv[..., GDN_WIDTH:2 * GDN_WIDTH].reshape(Bn, L, GDN_HEADS, GDN_HEAD_DIM))
    v = qkv[..., 2 * GDN_WIDTH:].reshape(Bn, L, GDN_HEADS, GDN_HEAD_DIM)
    g = -jnp.exp(a_log) * jax.nn.softplus(a_raw + dt_bias)
    beta = jax.nn.sigmoid(b_raw)
    o_f = _gated_delta_chunked(q, k, v, g[:, :, 0], beta[:, :, 0])
    o_b = _flip(_gated_delta_chunked(_flip(q), _flip(k), _flip(v), _flip(g[:, :, 1]), _flip(beta[:, :, 1])))
    o = _rmsnorm(o_f + o_b, norm_w) * jax.nn.silu(z)
    return o.reshape(Bn, L, GDN_WIDTH)


def fourier_mixer(u, w, b):
    Bn, L, _ = u.shape
    uh = u.reshape(Bn, L, FNET_HEADS, FNET_HEAD_DIM)
    f = jnp.fft.fft2(uh, axes=(1, 3), norm='ortho').real
    return f.reshape(Bn, L, FNET_WIDTH) @ w + b


def _lin_rec(e1, e2):
    a1, b1 = e1
    a2, b2 = e2
    return a1 * a2, a2 * b1 + b2


def s5_mixer(u, lam_re, lam_im, log_step, b_re, b_im, c_re, c_im, d_skip, glu_w, glu_b):
    Bn, L, _ = u.shape
    ug = u.reshape(Bn, L, S5_GROUPS, S5_GROUP_CH)
    bu = lax.complex(jnp.einsum('blgc,gpc->blgp', ug, b_re), jnp.einsum('blgc,gpc->blgp', ug, b_im))
    lam = lax.complex(lam_re, lam_im)
    delta = jnp.exp(log_step)[..., None]
    lam_bar = jnp.exp(lam * delta)
    b_scale = (lam_bar - 1.0) / lam
    c = lax.complex(c_re, c_im)
    y = u * d_skip
    for direction, reverse in ((0, False), (1, True)):
        a = jnp.broadcast_to(lam_bar[direction], (1, L, S5_GROUPS, S5_STATE))
        b = bu * b_scale[direction]
        if reverse:
            _, states = lax.associative_scan(_lin_rec, (a, _flip(b)), axis=1)
            states = _flip(states)
        else:
            _, states = lax.associative_scan(_lin_rec, (a, b), axis=1)
        y = y + jnp.einsum('blgp,gcp->blgc', states, c[direction]).real.reshape(Bn, L, S5_WIDTH)
    h = jax.nn.gelu(y)
    return h * jax.nn.sigmoid(h @ glu_w + glu_b)


def mixers(u_wide, u_narrow, batch, p, i):
    t = u_wide.shape[0]
    L = t // batch
    uw = u_wide.reshape(batch, L, U_WIDE)
    un = u_narrow.reshape(batch, L, LANES)
    y_ssd = ssd_mixer(uw[..., U_SSD_Z:U_SSD_XBC], uw[..., U_SSD_XBC:U_GDN_QKV], un[..., :2 * SSD_HEADS],
                      p['ssd_conv_w'][i], p['ssd_conv_b'][i], p['ssd_dt_bias'][i], p['ssd_a_log'][i], p['ssd_d'][i],
                      p['ssd_norm'][i])
    a0 = 2 * SSD_HEADS
    y_gdn = gdn_mixer(uw[..., U_GDN_QKV:U_GDN_Z], uw[..., U_GDN_Z:U_FNET], un[..., a0:a0 + 2 * GDN_HEADS],
                      un[..., a0 + 2 * GDN_HEADS:a0 + 4 * GDN_HEADS], p['gdn_conv_w'][i], p['gdn_dt_bias'][i],
                      p['gdn_a_log'][i], p['gdn_norm'][i])
    y_fnet = fourier_mixer(uw[..., U_FNET:U_S5], p['fnet_w'][i], p['fnet_b'][i])
    y_s5 = s5_mixer(uw[..., U_S5:U_WIDE], p['s5_lambda_re'][i], p['s5_lambda_im'][i], p['s5_log_step'][i],
                    p['s5_b_re'][i], p['s5_b_im'][i], p['s5_c_re'][i], p['s5_c_im'][i], p['s5_d'][i],
                    p['s5_glu_w'][i], p['s5_glu_b'][i])
    return jnp.concatenate([y_ssd, y_gdn, y_fnet, y_s5], axis=-1).reshape(t, 4 * GROUP_WIDTH)


def _split_w_in(w_in):
    o1 = SSD_IN
    o2 = o1 + GDN_IN
    ssd_z = w_in[..., :SSD_WIDTH]
    ssd_xbc = w_in[..., SSD_WIDTH:SSD_WIDTH + SSD_CONV_DIM]
    ssd_dt = w_in[..., SSD_WIDTH + SSD_CONV_DIM:o1]
    gdn_qkv = w_in[..., o1:o1 + GDN_CONV_DIM]
    gdn_z = w_in[..., o1 + GDN_CONV_DIM:o1 + 4 * GDN_WIDTH]
    gdn_ab = w_in[..., o1 + 4 * GDN_WIDTH:o2]
    rest = w_in[..., o2:]
    wide = jnp.concatenate([ssd_z, ssd_xbc, gdn_qkv, gdn_z, rest], axis=-1).astype(BF16)
    pad = jnp.zeros(w_in.shape[:-1] + (LANES - NARROW,), w_in.dtype)
    narrow = jnp.concatenate([ssd_dt, gdn_ab, pad], axis=-1).astype(BF16)
    return wide, narrow


def kernel(x_prompt, x_sample, norm_mix, w_in, ssd_conv_w, ssd_conv_b, ssd_dt_bias, ssd_a_log, ssd_d, ssd_norm,
           gdn_conv_w, gdn_dt_bias, gdn_a_log, gdn_norm, fnet_w, fnet_b, s5_lambda_re, s5_lambda_im, s5_log_step,
           s5_b_re, s5_b_im, s5_c_re, s5_c_im, s5_d, s5_glu_w, s5_glu_b, w_out, norm_ffn, ffn_w_gate, ffn_w_up,
           ffn_w_down, router_w, moe_w_gate, moe_w_up, moe_w_down, norm_final):
    p = dict(ssd_conv_w=ssd_conv_w, ssd_conv_b=ssd_conv_b, ssd_dt_bias=ssd_dt_bias, ssd_a_log=ssd_a_log, ssd_d=ssd_d,
             ssd_norm=ssd_norm, gdn_conv_w=gdn_conv_w, gdn_dt_bias=gdn_dt_bias, gdn_a_log=gdn_a_log,
             gdn_norm=gdn_norm, fnet_w=fnet_w, fnet_b=fnet_b, s5_lambda_re=s5_lambda_re, s5_lambda_im=s5_lambda_im,
             s5_log_step=s5_log_step, s5_b_re=s5_b_re, s5_b_im=s5_b_im, s5_c_re=s5_c_re, s5_c_im=s5_c_im, s5_d=s5_d,
             s5_glu_w=s5_glu_w, s5_glu_b=s5_glu_b)
    bp, lp, d = x_prompt.shape
    bs, ls, _ = x_sample.shape
    tp = bp * lp
    x = jnp.concatenate([x_prompt.reshape(tp, d), x_sample.reshape(bs * ls, d)], axis=0)
    w_wide, w_narrow = _split_w_in(w_in)
    w_out_b = w_out.astype(BF16)
    ffn_w_gate, ffn_w_up, ffn_w_down = (w.astype(BF16) for w in (ffn_w_gate, ffn_w_up, ffn_w_down))
    depth = w_in.shape[0]
    for i in range(depth):
        u_wide, u_narrow = norm_inproj(x, norm_mix[i], w_wide[i], w_narrow[i])
        y = jnp.concatenate([mixers(u_wide[:tp], u_narrow[:tp], bp, p, i),
                             mixers(u_wide[tp:], u_narrow[tp:], bs, p, i)], axis=0)
        x = outproj_residual(y, w_out_b[i], x)
        j = i // 2
        if i % 2 == 0:
            x = dense_ffn(x, norm_ffn[i], ffn_w_gate, ffn_w_up, ffn_w_down, j)
        else:
            x = moe_block(x, norm_ffn[i], router_w[j], moe_w_gate, moe_w_up, moe_w_down, j)
    y = final_norm(x, norm_final)
    return y[:tp].reshape(bp, lp, d), y[tp:].reshape(bs, ls, d)
```

```python
import functools
import math

import jax
import jax.numpy as jnp
from jax import lax
from jax.experimental import pallas as pl
from jax.experimental.pallas import tpu as pltpu

F32 = jnp.float32
BF16 = jnp.bfloat16

D_MODEL = 2048
DEPTH = 4
GROUP_WIDTH = 512

SSD_WIDTH = GROUP_WIDTH
SSD_HEAD_DIM = 64
SSD_HEADS = SSD_WIDTH // SSD_HEAD_DIM
SSD_GROUPS = 2
SSD_STATE = 128
SSD_CHUNK = 128
SSD_CONV_DIM = SSD_WIDTH + 2 * SSD_GROUPS * SSD_STATE

GDN_WIDTH = GROUP_WIDTH
GDN_HEAD_DIM = 128
GDN_HEADS = GDN_WIDTH // GDN_HEAD_DIM
GDN_CHUNK = 64
GDN_CONV_DIM = 3 * GDN_WIDTH

CONV_K = 5

FNET_WIDTH = GROUP_WIDTH
FNET_HEADS = 4
FNET_HEAD_DIM = FNET_WIDTH // FNET_HEADS

S5_WIDTH = GROUP_WIDTH
S5_GROUP_CH = 16
S5_GROUPS = S5_WIDTH // S5_GROUP_CH
S5_STATE = 64

SSD_IN = 2 * SSD_WIDTH + 2 * SSD_GROUPS * SSD_STATE + 2 * SSD_HEADS
GDN_IN = 4 * GDN_WIDTH + 4 * GDN_HEADS
IN_COLS = SSD_IN + GDN_IN + FNET_WIDTH + S5_WIDTH

N_EXPERTS = 8
TOP_K = 2
EPS = 1e-6

LANES = 128
VMEM_LIMIT_BYTES = 56 * 1024 * 1024

U_SSD_Z = 0
U_SSD_XBC = U_SSD_Z + SSD_WIDTH
U_GDN_QKV = U_SSD_XBC + SSD_CONV_DIM
U_GDN_Z = U_GDN_QKV + GDN_CONV_DIM
U_FNET = U_GDN_Z + GDN_WIDTH
U_S5 = U_FNET + FNET_WIDTH
U_WIDE = U_S5 + S5_WIDTH
NARROW = 2 * SSD_HEADS + 4 * GDN_HEADS

ROW_TILE = 1024
FF_TILE = 256


def _params(semantics):
    return pltpu.CompilerParams(dimension_semantics=semantics, vmem_limit_bytes=VMEM_LIMIT_BYTES)


def _rms_normalize(x, gain):
    ms = jnp.mean(x * x, axis=-1, keepdims=True)
    return x * lax.rsqrt(ms + EPS) * gain


def _norm_inproj_kernel(x_ref, g_ref, w_ref, wn_ref, o_ref, on_ref, h_ref):
    @pl.when(pl.program_id(1) == 0)
    def _():
        h = _rms_normalize(x_ref[...], g_ref[...]).astype(BF16)
        h_ref[...] = h
        on_ref[...] = jnp.dot(h, wn_ref[...], preferred_element_type=F32)

    o_ref[...] = jnp.dot(h_ref[...], w_ref[...], preferred_element_type=F32)


def norm_inproj(x, gain, w_wide, w_narrow, *, tm=ROW_TILE, tn=512):
    t, d = x.shape
    n = w_wide.shape[1]
    return pl.pallas_call(
        _norm_inproj_kernel,
        grid=(t // tm, n // tn),
        in_specs=[
            pl.BlockSpec((tm, d), lambda i, j: (i, 0)),
            pl.BlockSpec((1, d), lambda i, j: (0, 0)),
            pl.BlockSpec((d, tn), lambda i, j: (0, j)),
            pl.BlockSpec((d, LANES), lambda i, j: (0, 0)),
        ],
        out_specs=[
            pl.BlockSpec((tm, tn), lambda i, j: (i, j)),
            pl.BlockSpec((tm, LANES), lambda i, j: (i, 0)),
        ],
        out_shape=[jax.ShapeDtypeStruct((t, n), F32), jax.ShapeDtypeStruct((t, LANES), F32)],
        scratch_shapes=[pltpu.VMEM((tm, d), BF16)],
        compiler_params=_params(("parallel", "arbitrary")),
        name="norm_inproj",
    )(x, gain.reshape(1, d), w_wide, w_narrow)


def _outproj_kernel(y_ref, w_ref, x_ref, o_ref):
    o_ref[...] = x_ref[...] + jnp.dot(y_ref[...].astype(BF16), w_ref[...], preferred_element_type=F32)


def outproj_residual(y, w, x, *, tm=512):
    t, d = x.shape
    k = y.shape[1]
    return pl.pallas_call(
        _outproj_kernel,
        grid=(t // tm,),
        in_specs=[
            pl.BlockSpec((tm, k), lambda i: (i, 0)),
            pl.BlockSpec((k, d), lambda i: (0, 0)),
            pl.BlockSpec((tm, d), lambda i: (i, 0)),
        ],
        out_specs=pl.BlockSpec((tm, d), lambda i: (i, 0)),
        out_shape=jax.ShapeDtypeStruct((t, d), F32),
        compiler_params=_params(("parallel",)),
        name="outproj_residual",
    )(y, w, x)


def _swiglu_step(h, wg_ref, wu_ref, wd_ref, acc_ref):
    g = jnp.dot(h, wg_ref[0].astype(BF16), preferred_element_type=F32)
    u = jnp.dot(h, wu_ref[0].astype(BF16), preferred_element_type=F32)
    a = (g * jax.nn.sigmoid(g) * u).astype(BF16)
    acc_ref[...] += jnp.dot(a, wd_ref[0].astype(BF16), preferred_element_type=F32)


def _dense_ffn_kernel(x_ref, g_ref, wg_ref, wu_ref, wd_ref, o_ref, h_ref, acc_ref):
    f = pl.program_id(1)

    @pl.when(f == 0)
    def _():
        h_ref[...] = _rms_normalize(x_ref[...], g_ref[...]).astype(BF16)
        acc_ref[...] = jnp.zeros_like(acc_ref)

    _swiglu_step(h_ref[...], wg_ref, wu_ref, wd_ref, acc_ref)

    @pl.when(f == pl.num_programs(1) - 1)
    def _():
        o_ref[...] = x_ref[...] + acc_ref[...]


def dense_ffn(x, gain, wg, wu, wd, layer, *, tm=512, tf=512):
    t, d = x.shape
    ff = wg.shape[2]
    return pl.pallas_call(
        _dense_ffn_kernel,
        grid=(t // tm, ff // tf),
        in_specs=[
            pl.BlockSpec((tm, d), lambda i, f: (i, 0)),
            pl.BlockSpec((1, d), lambda i, f: (0, 0)),
            pl.BlockSpec((1, d, tf), lambda i, f: (layer, 0, f)),
            pl.BlockSpec((1, d, tf), lambda i, f: (layer, 0, f)),
            pl.BlockSpec((1, tf, d), lambda i, f: (layer, f, 0)),
        ],
        out_specs=pl.BlockSpec((tm, d), lambda i, f: (i, 0)),
        out_shape=jax.ShapeDtypeStruct((t, d), F32),
        scratch_shapes=[pltpu.VMEM((tm, d), BF16), pltpu.VMEM((tm, d), F32)],
        compiler_params=_params(("parallel", "arbitrary")),
        name="dense_ffn",
    )(x, gain.reshape(1, d), wg, wu, wd)


def _moe_ffn_kernel(te_ref, na_ref, x_ref, s_ref, wg_ref, wu_ref, wd_ref, o_ref, acc_ref):
    i = pl.program_id(0)
    f = pl.program_id(1)

    @pl.when(i < na_ref[0])
    def _():
        @pl.when(f == 0)
        def _():
            acc_ref[...] = jnp.zeros_like(acc_ref)

        _swiglu_step(x_ref[...], wg_ref, wu_ref, wd_ref, acc_ref)

        @pl.when(f == pl.num_programs(1) - 1)
        def _():
            o_ref[...] = acc_ref[...] * s_ref[...]


def moe_ffn_grouped(xs, scale, tile_expert, n_active, wg, wu, wd, layer, *, tm=ROW_TILE, tf=FF_TILE):
    p, d = xs.shape
    n_e, ff = wg.shape[1], wg.shape[3]
    n_f = ff // tf
    wg = wg.reshape(-1, d, ff)
    wu = wu.reshape(-1, d, ff)
    wd = wd.reshape(-1, ff, d)

    def row_idx(i, f, te, na):
        return (jnp.minimum(i, na[0] - 1), 0)

    def up_idx(i, f, te, na):
        live = i < na[0]
        return (layer * n_e + te[jnp.minimum(i, na[0] - 1)], 0, jnp.where(live, f, n_f - 1))

    def down_idx(i, f, te, na):
        live = i < na[0]
        return (layer * n_e + te[jnp.minimum(i, na[0] - 1)], jnp.where(live, f, n_f - 1), 0)

    grid_spec = pltpu.PrefetchScalarGridSpec(
        num_scalar_prefetch=2,
        grid=(p // tm, n_f),
        in_specs=[
            pl.BlockSpec((tm, d), row_idx),
            pl.BlockSpec((tm, 1), row_idx),
            pl.BlockSpec((1, d, tf), up_idx),
            pl.BlockSpec((1, d, tf), up_idx),
            pl.BlockSpec((1, tf, d), down_idx),
        ],
        out_specs=pl.BlockSpec((tm, d), row_idx),
        scratch_shapes=[pltpu.VMEM((tm, d), F32)],
    )
    return pl.pallas_call(
        _moe_ffn_kernel,
        grid_spec=grid_spec,
        out_shape=jax.ShapeDtypeStruct((p, d), F32),
        compiler_params=_params(("arbitrary", "arbitrary")),
        name="moe_ffn",
    )(tile_expert, n_active, xs, scale, wg, wu, wd)


def _norm_router_kernel(x_ref, g_ref, wr_ref, h_ref, l_ref):
    h = _rms_normalize(x_ref[...], g_ref[...])
    h_ref[...] = h.astype(BF16)
    l_ref[...] = jnp.dot(h, wr_ref[...], preferred_element_type=F32, precision=lax.Precision.HIGHEST)


def norm_router(x, gain, router_w_padded, *, tm=512):
    t, d = x.shape
    return pl.pallas_call(
        _norm_router_kernel,
        grid=(t // tm,),
        in_specs=[
            pl.BlockSpec((tm, d), lambda i: (i, 0)),
            pl.BlockSpec((1, d), lambda i: (0, 0)),
            pl.BlockSpec((d, LANES), lambda i: (0, 0)),
        ],
        out_specs=[pl.BlockSpec((tm, d), lambda i: (i, 0)), pl.BlockSpec((tm, LANES), lambda i: (i, 0))],
        out_shape=[jax.ShapeDtypeStruct((t, d), BF16), jax.ShapeDtypeStruct((t, LANES), F32)],
        compiler_params=_params(("parallel",)),
        name="norm_router",
    )(x, gain.reshape(1, d), router_w_padded)


def _final_norm_kernel(x_ref, g_ref, o_ref):
    o_ref[...] = _rms_normalize(x_ref[...], g_ref[...])


def final_norm(x, gain, *, tm=512):
    t, d = x.shape
    return pl.pallas_call(
        _final_norm_kernel,
        grid=(t // tm,),
        in_specs=[pl.BlockSpec((tm, d), lambda i: (i, 0)), pl.BlockSpec((1, d), lambda i: (0, 0))],
        out_specs=pl.BlockSpec((tm, d), lambda i: (i, 0)),
        out_shape=jax.ShapeDtypeStruct((t, d), F32),
        compiler_params=_params(("parallel",)),
        name="final_norm",
    )(x, gain.reshape(1, d))


def moe_block(x, gain, router_w, wg, wu, wd, layer, *, tm=ROW_TILE, tf=FF_TILE):
    t, d = x.shape
    n_e = router_w.shape[1]
    wr = jnp.zeros((d, LANES), F32).at[:, :n_e].set(router_w)
    h, logits = norm_router(x, gain, wr)
    top_logit, top_idx = lax.top_k(logits[:, :n_e], TOP_K)
    top_w = jax.nn.softmax(top_logit, axis=-1)

    n_assign = t * TOP_K
    n_tiles = (n_assign + n_e * (tm - 1)) // tm + 1
    flat_e = top_idx.reshape(-1)
    order = jnp.argsort(flat_e, stable=True)
    sorted_e = flat_e[order]
    counts = jnp.zeros((n_e,), jnp.int32).at[flat_e].add(1)
    padded = ((counts + tm - 1) // tm) * tm
    pad_end = jnp.cumsum(padded)
    pad_start = pad_end - padded
    start = jnp.cumsum(counts) - counts
    dest = pad_start[sorted_e] + jnp.arange(n_assign, dtype=jnp.int32) - start[sorted_e]
    src_token = jnp.zeros((n_tiles * tm,), jnp.int32).at[dest].set((order // TOP_K).astype(jnp.int32))
    scale = jnp.zeros((n_tiles * tm,), F32).at[dest].set(top_w.reshape(-1)[order])
    pos = jnp.zeros((n_assign,), jnp.int32).at[order].set(dest).reshape(t, TOP_K)
    tile_start = jnp.arange(n_tiles, dtype=jnp.int32) * tm
    tile_expert = jnp.minimum(jnp.searchsorted(pad_end, tile_start, side="right"), n_e - 1).astype(jnp.int32)
    n_active = (pad_end[-1] // tm).astype(jnp.int32).reshape(1)

    xs = jnp.take(h, src_token, axis=0)
    ys = moe_ffn_grouped(xs, scale.reshape(-1, 1), tile_expert, n_active, wg, wu, wd, layer, tm=tm, tf=tf)
    return x + jnp.take(ys, pos[:, 0], axis=0) + jnp.take(ys, pos[:, 1], axis=0)


def _rmsnorm(x, w):
    return x * lax.rsqrt(jnp.mean(jnp.square(x), axis=-1, keepdims=True) + EPS) * w


def _l2norm(t):
    return t * lax.rsqrt(jnp.sum(jnp.square(t), axis=-1, keepdims=True) + EPS)


def _flip(t):
    return jnp.flip(t, axis=1)


def _centred_dwconv(x, w):
    rhs = jnp.transpose(w)[:, None, :].astype(x.dtype)
    return lax.conv_general_dilated(x, rhs, window_strides=(1,), padding=[(CONV_K // 2, CONV_K // 2)],
                                    dimension_numbers=('NWC', 'WIO', 'NWC'), feature_group_count=x.shape[-1])


def _ssd_scan(x, dt, a, bm, cm):
    Bn, L, H, P = x.shape
    N = bm.shape[-1]
    Q = SSD_CHUNK
    c = L // Q
    xdt = (x * dt[..., None]).reshape(Bn, c, Q, H, P)
    bm = bm.reshape(Bn, c, Q, H, N)
    cm = cm.reshape(Bn, c, Q, H, N)
    acum = jnp.cumsum(jnp.moveaxis((dt * a).reshape(Bn, c, Q, H), -1, 2), axis=-1)
    causal = jnp.tril(jnp.ones((Q, Q), dtype=bool))
    seg = jnp.exp(jnp.where(causal, acum[..., :, None] - acum[..., None, :], -jnp.inf))
    scores = jnp.einsum('bclhn,bcshn->bchls', cm, bm) * seg
    y_diag = jnp.einsum('bchls,bcshp->bclhp', scores, xdt)
    decay_to_end = jnp.exp(acum[..., -1:] - acum)
    chunk_states = jnp.einsum('bclhn,bchl,bclhp->bchpn', bm, decay_to_end, xdt)
    chunk_decay = jnp.exp(acum[..., -1])

    def step(h, inp):
        s, d = inp
        return h * d[..., None, None] + s, h

    h0 = jnp.zeros((Bn, H, P, N), x.dtype)
    _, prev = lax.scan(step, h0, (jnp.moveaxis(chunk_states, 1, 0), jnp.moveaxis(chunk_decay, 1, 0)))
    prev = jnp.moveaxis(prev, 0, 1)
    y_off = jnp.einsum('bclhn,bchpn,bchl->bclhp', cm, prev, jnp.exp(acum))
    return (y_diag + y_off).reshape(Bn, L, H, P)


def ssd_mixer(z, xbc, dt_raw, conv_w, conv_b, dt_bias, a_log, d_skip, norm_w):
    Bn, L, _ = z.shape
    xbc = jax.nn.silu(_centred_dwconv(xbc, conv_w) + conv_b)
    x = xbc[..., :SSD_WIDTH].reshape(Bn, L, SSD_HEADS, SSD_HEAD_DIM)
    rep = SSD_HEADS // SSD_GROUPS
    bm = jnp.repeat(xbc[..., SSD_WIDTH:SSD_WIDTH + SSD_GROUPS * SSD_STATE].reshape(Bn, L, SSD_GROUPS, SSD_STATE), rep, axis=2)
    cm = jnp.repeat(xbc[..., SSD_WIDTH + SSD_GROUPS * SSD_STATE:].reshape(Bn, L, SSD_GROUPS, SSD_STATE), rep, axis=2)
    dt = jax.nn.softplus(dt_raw.reshape(Bn, L, 2, SSD_HEADS) + dt_bias)
    a = -jnp.exp(a_log)
    y_f = _ssd_scan(x, dt[:, :, 0], a[0], bm, cm)
    y_b = _flip(_ssd_scan(_flip(x), _flip(dt[:, :, 1]), a[1], _flip(bm), _flip(cm)))
    y = (y_f + y_b + x * d_skip[:, None]).reshape(Bn, L, SSD_WIDTH)
    return _rmsnorm(y * jax.nn.silu(z), norm_w)


def _gated_delta_chunked(q, k, v, g, beta):
    Bn, L, H, K = q.shape
    V = v.shape[-1]
    Q = GDN_CHUNK
    c = L // Q

    def chunks(t):
        return jnp.swapaxes(t.reshape(Bn, c, Q, H, *t.shape[3:]), 2, 3)

    q, k, v, g, beta = chunks(q), chunks(k), chunks(v), chunks(g), chunks(beta)
    gc = jnp.cumsum(g, axis=-1)
    causal = jnp.tril(jnp.ones((Q, Q), dtype=bool))
    strict = jnp.tril(jnp.ones((Q, Q), dtype=bool), -1)
    decay = jnp.exp(jnp.where(causal, gc[..., :, None] - gc[..., None, :], -jnp.inf))
    kb = k * beta[..., None]
    lower = jnp.where(strict, jnp.einsum('bchik,bchjk->bchij', kb, k) * decay, 0.0)
    eye = jnp.eye(Q, dtype=q.dtype)
    rhs = jnp.concatenate([v * beta[..., None], kb * jnp.exp(gc)[..., None]], axis=-1)
    sol = lax.linalg.triangular_solve(eye + lower, rhs, left_side=True, lower=True, unit_diagonal=True)
    u_c, w_c = sol[..., :V], sol[..., V:]
    intra = jnp.einsum('bchik,bchjk->bchij', q, k) * decay

    def step(S, inp):
        q_i, k_i, u_i, w_i, gc_i, a_i = inp
        v_new = u_i - jnp.einsum('bhqk,bhkv->bhqv', w_i, S)
        o = jnp.einsum('bhqk,bhkv->bhqv', q_i * jnp.exp(gc_i)[..., None], S) + jnp.einsum('bhij,bhjv->bhiv', a_i, v_new)
        g_last = gc_i[..., -1]
        S = S * jnp.exp(g_last)[..., None, None] + jnp.einsum(
            'bhqk,bhqv->bhkv', k_i * jnp.exp(g_last[..., None] - gc_i)[..., None], v_new)
        return S, o

    xs = tuple(jnp.moveaxis(t, 1, 0) for t in (q, k, u_c, w_c, gc, intra))
    S0 = jnp.zeros((Bn, H, K, V), q.dtype)
    _, o = lax.scan(step, S0, xs)
    return jnp.transpose(o, (1, 0, 3, 2, 4)).reshape(Bn, L, H, V)


def gdn_mixer(qkv, z, a_raw, b_raw, conv_w, dt_bias, a_log, norm_w):
    Bn, L, _ = z.shape
    qkv = jax.nn.silu(_centred_dwconv(qkv, conv_w))
    z = z.reshape(Bn, L, GDN_HEADS, GDN_HEAD_DIM)
    a_raw = a_raw.reshape(Bn, L, 2, GDN_HEADS)
    b_raw = b_raw.reshape(Bn, L, 2, GDN_HEADS)
    q = _l2norm(qkv[..., :GDN_WIDTH].reshape(Bn, L, GDN_HEADS, GDN_HEAD_DIM)) * (GDN_HEAD_DIM ** -0.5)
    k = _l2norm(qkv[..., GDN_WIDTH:2 * GDN_WIDTH].reshape(Bn, L, GDN_HEADS, GDN_HEAD_DIM))
    v = qkv[..., 2 * GDN_WIDTH:].reshape(Bn, L, GDN_HEADS, GDN_HEAD_DIM)
    g = -jnp.exp(a_log) * jax.nn.softplus(a_raw + dt_bias)
    beta = jax.nn.sigmoid(b_raw)
    o_f = _gated_delta_chunked(q, k, v, g[:, :, 0], beta[:, :, 0])
    o_b = _flip(_gated_delta_chunked(_flip(q), _flip(k), _flip(v), _flip(g[:, :, 1]), _flip(beta[:, :, 1])))
    o = _rmsnorm(o_f + o_b, norm_w) * jax.nn.silu(z)
    return o.reshape(Bn, L, GDN_WIDTH)


def fourier_mixer(u, w, b):
    Bn, L, _ = u.shape
    uh = u.reshape(Bn, L, FNET_HEADS, FNET_HEAD_DIM)
    f = jnp.fft.fft2(uh, axes=(1, 3), norm='ortho').real
    return f.reshape(Bn, L, FNET_WIDTH) @ w + b


S5_TILE = 256
S5_NSTATE = S5_GROUPS * S5_STATE
S5_COLS = 512
S5_BLOCKS = S5_WIDTH // LANES


def _s5_scan_kernel(u_ref, pi_ref, pit_ref, b_ref, c_ref, lre_ref, lim_ref, o_ref,
                    sre, sim, pre, pim, car_re, car_im, e_re, e_im, cin_re, cin_im):
    tl = u_ref.shape[0]
    steps = tl // 8
    first_tile = pl.program_id(2) == 0

    def col(j):
        return slice(j * S5_COLS, (j + 1) * S5_COLS)

    def rows(t):
        return pl.ds(pl.multiple_of(t * 8, 8), 8)

    def bcast(ref, j):
        return jnp.broadcast_to(ref[0, :, col(j)], (8, S5_COLS))

    @pl.when(first_tile)
    def _():
        car_re[...] = jnp.zeros_like(car_re)
        car_im[...] = jnp.zeros_like(car_im)
        for j in range(S5_BLOCKS):
            lr, li = bcast(lre_ref, j), bcast(lim_ref, j)

            def power_step(t, p, j=j, lr=lr, li=li):
                pr, pi_ = p
                pre[rows(t), col(j)] = pr
                pim[rows(t), col(j)] = pi_
                return pr * lr - pi_ * li, pr * li + pi_ * lr

            lax.fori_loop(0, steps, power_step, (lr, li))

    up = jnp.dot(pi_ref[0], u_ref[...].astype(BF16), preferred_element_type=F32).astype(BF16)
    for j in range(S5_BLOCKS):
        uj = up[:, j * LANES:(j + 1) * LANES]
        sre[:, col(j)] = jnp.dot(uj, b_ref[0, j, 0], preferred_element_type=F32)
        sim[:, col(j)] = jnp.dot(uj, b_ref[0, j, 1], preferred_element_type=F32)

    for j in range(S5_BLOCKS):
        lr, li = bcast(lre_ref, j), bcast(lim_ref, j)

        def scan_step(t, s, j=j, lr=lr, li=li):
            sr, si = s
            nr = lr * sr - li * si + sre[rows(t), col(j)]
            ni = lr * si + li * sr + sim[rows(t), col(j)]
            sre[rows(t), col(j)] = nr
            sim[rows(t), col(j)] = ni
            return nr, ni

        zero = jnp.zeros((8, S5_COLS), F32)
        er, ei = lax.fori_loop(0, steps, scan_step, (zero, zero), unroll=4)
        e_re[:, col(j)] = er
        e_im[:, col(j)] = ei

    last = pl.ds((steps - 1) * 8, 1)
    lsr, lsi = pre[last, :], pim[last, :]
    cr, ci = car_re[...], car_im[...]
    for r in range(8):
        cin_re[r:r + 1, :] = cr
        cin_im[r:r + 1, :] = ci
        cr, ci = (e_re[r:r + 1, :] + lsr * cr - lsi * ci, e_im[r:r + 1, :] + lsr * ci + lsi * cr)
    car_re[...] = cr
    car_im[...] = ci

    for j in range(S5_BLOCKS):
        cr, ci = cin_re[:, col(j)], cin_im[:, col(j)]

        def fix_step(t, carry, j=j, cr=cr, ci=ci):
            pr, pi_ = pre[rows(t), col(j)], pim[rows(t), col(j)]
            sre[rows(t), col(j)] = sre[rows(t), col(j)] + pr * cr - pi_ * ci
            sim[rows(t), col(j)] = sim[rows(t), col(j)] + pr * ci + pi_ * cr
            return carry

        lax.fori_loop(0, steps, fix_step, 0, unroll=4)

    ys = []
    for j in range(S5_BLOCKS):
        ys.append(jnp.dot(sre[:, col(j)].astype(BF16), c_ref[0, j, 0], preferred_element_type=F32)
                  + jnp.dot(sim[:, col(j)].astype(BF16), c_ref[0, j, 1], preferred_element_type=F32))
    y = jnp.concatenate(ys, axis=-1)
    y_hi = y.astype(BF16)
    y_lo = (y - y_hi.astype(F32)).astype(BF16)
    pit = pit_ref[0]
    o_ref[0] = jnp.dot(pit, y_hi, preferred_element_type=F32) + jnp.dot(pit, y_lo, preferred_element_type=F32)


def _s5_operators(lam_re, lam_im, log_step, b_re, b_im, c_re, c_im, tl):
    lam = lax.complex(lam_re, lam_im)
    delta = jnp.exp(log_step)[..., None]
    lam_bar = jnp.exp(lam * delta)
    b_scale = (lam_bar - 1.0) / lam
    eye = jnp.eye(8, dtype=F32)
    gpb = S5_GROUPS // S5_BLOCKS
    bd = lax.complex(b_re, b_im)[None] * b_scale[..., None]
    bd = bd.reshape(2, S5_BLOCKS, gpb, S5_STATE, S5_GROUP_CH)
    bmat = jnp.stack([jnp.einsum('djgpc,gh->djgchp', part, eye).reshape(2, S5_BLOCKS, LANES, S5_COLS)
                      for part in (bd.real, bd.imag)], axis=2).astype(BF16)
    cd = lax.complex(c_re, c_im).reshape(2, S5_BLOCKS, gpb, S5_GROUP_CH, S5_STATE)
    cmat = jnp.stack([jnp.einsum('djgcp,gh->djhpgc', part, eye).reshape(2, S5_BLOCKS, S5_COLS, LANES)
                      for part in (cd.real, -cd.imag)], axis=2).astype(BF16)
    lre = lam_bar.real.reshape(2, 1, S5_NSTATE)
    lim = lam_bar.imag.reshape(2, 1, S5_NSTATE)
    steps = tl // 8
    rt = jnp.arange(tl)
    src = (rt % 8) * steps + rt // 8
    fwd = (src[:, None] == rt[None, :])
    bwd = ((tl - 1 - src)[:, None] == rt[None, :])
    pi = jnp.stack([fwd, bwd]).astype(BF16)
    return pi, jnp.swapaxes(pi, 1, 2), bmat, cmat, lre, lim


def s5_scan(u_wide, ops, batch, *, tl=S5_TILE):
    pi, pit, bmat, cmat, lre, lim = ops
    t = u_wide.shape[0]
    nt = t // batch // tl
    ublk = U_S5 // S5_WIDTH

    def tile_idx(b, d, i):
        return b * nt + jnp.where(d == 0, i, nt - 1 - i)

    state = pltpu.VMEM((tl, S5_NSTATE), F32)
    row = pltpu.VMEM((1, S5_NSTATE), F32)
    seg = pltpu.VMEM((8, S5_NSTATE), F32)
    return pl.pallas_call(
        _s5_scan_kernel,
        grid=(batch, 2, nt),
        in_specs=[
            pl.BlockSpec((tl, S5_WIDTH), lambda b, d, i: (tile_idx(b, d, i), ublk)),
            pl.BlockSpec((1, tl, tl), lambda b, d, i: (d, 0, 0)),
            pl.BlockSpec((1, tl, tl), lambda b, d, i: (d, 0, 0)),
            pl.BlockSpec((1, S5_BLOCKS, 2, LANES, S5_COLS), lambda b, d, i: (d, 0, 0, 0, 0)),
            pl.BlockSpec((1, S5_BLOCKS, 2, S5_COLS, LANES), lambda b, d, i: (d, 0, 0, 0, 0)),
            pl.BlockSpec((1, 1, S5_NSTATE), lambda b, d, i: (d, 0, 0)),
            pl.BlockSpec((1, 1, S5_NSTATE), lambda b, d, i: (d, 0, 0)),
        ],
        out_specs=pl.BlockSpec((1, tl, S5_WIDTH), lambda b, d, i: (d, tile_idx(b, d, i), 0)),
        out_shape=jax.ShapeDtypeStruct((2, t, S5_WIDTH), F32),
        scratch_shapes=[state, state, state, state, row, row, seg, seg, seg, seg],
        compiler_params=_params(("arbitrary", "arbitrary", "arbitrary")),
        name="s5_scan",
    )(u_wide, pi, pit, bmat, cmat, lre, lim)


def _s5_gate_kernel(y_ref, u_ref, d_ref, w_ref, b_ref, o_ref):
    y = y_ref[0] + y_ref[1] + u_ref[...] * d_ref[...]
    h = jax.nn.gelu(y)
    gate = jnp.dot(h.astype(BF16), w_ref[...], preferred_element_type=F32) + b_ref[...]
    o_ref[...] = h * jax.nn.sigmoid(gate)


def s5_gate(y2, u_wide, d_skip, glu_w, glu_b, *, tm=512):
    t = u_wide.shape[0]
    ublk = U_S5 // S5_WIDTH
    return pl.pallas_call(
        _s5_gate_kernel,
        grid=(t // tm,),
        in_specs=[
            pl.BlockSpec((2, tm, S5_WIDTH), lambda i: (0, i, 0)),
            pl.BlockSpec((tm, S5_WIDTH), lambda i: (i, ublk)),
            pl.BlockSpec((1, S5_WIDTH), lambda i: (0, 0)),
            pl.BlockSpec((S5_WIDTH, S5_WIDTH), lambda i: (0, 0)),
            pl.BlockSpec((1, S5_WIDTH), lambda i: (0, 0)),
        ],
        out_specs=pl.BlockSpec((tm, S5_WIDTH), lambda i: (i, 0)),
        out_shape=jax.ShapeDtypeStruct((t, S5_WIDTH), F32),
        compiler_params=_params(("parallel",)),
        name="s5_gate",
    )(y2, u_wide, d_skip.reshape(1, -1), glu_w.astype(BF16), glu_b.reshape(1, -1))


def s5_mixer(u_wide, batch, ops, d_skip, glu_w, glu_b):
    return s5_gate(s5_scan(u_wide, ops, batch), u_wide, d_skip, glu_w, glu_b)


def mixers(u_wide, u_narrow, batch, p, i):
    t = u_wide.shape[0]
    L = t // batch
    uw = u_wide.reshape(batch, L, U_WIDE)
    un = u_narrow.reshape(batch, L, LANES)
    y_ssd = ssd_mixer(uw[..., U_SSD_Z:U_SSD_XBC], uw[..., U_SSD_XBC:U_GDN_QKV], un[..., :2 * SSD_HEADS],
                      p['ssd_conv_w'][i], p['ssd_conv_b'][i], p['ssd_dt_bias'][i], p['ssd_a_log'][i], p['ssd_d'][i],
                      p['ssd_norm'][i])
    a0 = 2 * SSD_HEADS
    y_gdn = gdn_mixer(uw[..., U_GDN_QKV:U_GDN_Z], uw[..., U_GDN_Z:U_FNET], un[..., a0:a0 + 2 * GDN_HEADS],
                      un[..., a0 + 2 * GDN_HEADS:a0 + 4 * GDN_HEADS], p['gdn_conv_w'][i], p['gdn_dt_bias'][i],
                      p['gdn_a_log'][i], p['gdn_norm'][i])
    y_fnet = fourier_mixer(uw[..., U_FNET:U_S5], p['fnet_w'][i], p['fnet_b'][i])
    s5_ops = _s5_operators(p['s5_lambda_re'][i], p['s5_lambda_im'][i], p['s5_log_step'][i], p['s5_b_re'][i],
                           p['s5_b_im'][i], p['s5_c_re'][i], p['s5_c_im'][i], S5_TILE)
    y_s5 = s5_mixer(u_wide, batch, s5_ops, p['s5_d'][i], p['s5_glu_w'][i], p['s5_glu_b'][i])
    y_seq = jnp.concatenate([y_ssd, y_gdn, y_fnet], axis=-1).reshape(t, 3 * GROUP_WIDTH)
    return jnp.concatenate([y_seq, y_s5], axis=-1)


def _split_w_in(w_in):
    o1 = SSD_IN
    o2 = o1 + GDN_IN
    ssd_z = w_in[..., :SSD_WIDTH]
    ssd_xbc = w_in[..., SSD_WIDTH:SSD_WIDTH + SSD_CONV_DIM]
    ssd_dt = w_in[..., SSD_WIDTH + SSD_CONV_DIM:o1]
    gdn_qkv = w_in[..., o1:o1 + GDN_CONV_DIM]
    gdn_z = w_in[..., o1 + GDN_CONV_DIM:o1 + 4 * GDN_WIDTH]
    gdn_ab = w_in[..., o1 + 4 * GDN_WIDTH:o2]
    rest = w_in[..., o2:]
    wide = jnp.concatenate([ssd_z, ssd_xbc, gdn_qkv, gdn_z, rest], axis=-1).astype(BF16)
    pad = jnp.zeros(w_in.shape[:-1] + (LANES - NARROW,), w_in.dtype)
    narrow = jnp.concatenate([ssd_dt, gdn_ab, pad], axis=-1).astype(BF16)
    return wide, narrow


def kernel(x_prompt, x_sample, norm_mix, w_in, ssd_conv_w, ssd_conv_b, ssd_dt_bias, ssd_a_log, ssd_d, ssd_norm,
           gdn_conv_w, gdn_dt_bias, gdn_a_log, gdn_norm, fnet_w, fnet_b, s5_lambda_re, s5_lambda_im, s5_log_step,
           s5_b_re, s5_b_im, s5_c_re, s5_c_im, s5_d, s5_glu_w, s5_glu_b, w_out, norm_ffn, ffn_w_gate, ffn_w_up,
           ffn_w_down, router_w, moe_w_gate, moe_w_up, moe_w_down, norm_final):
    p = dict(ssd_conv_w=ssd_conv_w, ssd_conv_b=ssd_conv_b, ssd_dt_bias=ssd_dt_bias, ssd_a_log=ssd_a_log, ssd_d=ssd_d,
             ssd_norm=ssd_norm, gdn_conv_w=gdn_conv_w, gdn_dt_bias=gdn_dt_bias, gdn_a_log=gdn_a_log,
             gdn_norm=gdn_norm, fnet_w=fnet_w, fnet_b=fnet_b, s5_lambda_re=s5_lambda_re, s5_lambda_im=s5_lambda_im,
             s5_log_step=s5_log_step, s5_b_re=s5_b_re, s5_b_im=s5_b_im, s5_c_re=s5_c_re, s5_c_im=s5_c_im, s5_d=s5_d,
             s5_glu_w=s5_glu_w, s5_glu_b=s5_glu_b)
    bp, lp, d = x_prompt.shape
    bs, ls, _ = x_sample.shape
    tp = bp * lp
    x = jnp.concatenate([x_prompt.reshape(tp, d), x_sample.reshape(bs * ls, d)], axis=0)
    w_wide, w_narrow = _split_w_in(w_in)
    w_out_b = w_out.astype(BF16)
    ffn_w_gate, ffn_w_up, ffn_w_down = (w.astype(BF16) for w in (ffn_w_gate, ffn_w_up, ffn_w_down))
    depth = w_in.shape[0]
    for i in range(depth):
        u_wide, u_narrow = norm_inproj(x, norm_mix[i], w_wide[i], w_narrow[i])
        y = jnp.concatenate([mixers(u_wide[:tp], u_narrow[:tp], bp, p, i),
                             mixers(u_wide[tp:], u_narrow[tp:], bs, p, i)], axis=0)
        x = outproj_residual(y, w_out_b[i], x)
        j = i // 2
        if i % 2 == 0:
            x = dense_ffn(x, norm_ffn[i], ffn_w_gate, ffn_w_up, ffn_w_down, j)
        else:
            x = moe_block(x, norm_ffn[i], router_w[j], moe_w_gate, moe_w_up, moe_w_down, j)
    y = final_norm(x, norm_final)
    return y[:tp].reshape(bp, lp, d), y[tp:].reshape(bs, ls, d)
```

```python
import functools
import math

import jax
import jax.numpy as jnp
from jax import lax
from jax.experimental import pallas as pl
from jax.experimental.pallas import tpu as pltpu

F32 = jnp.float32
BF16 = jnp.bfloat16

D_MODEL = 2048
DEPTH = 4
GROUP_WIDTH = 512

SSD_WIDTH = GROUP_WIDTH
SSD_HEAD_DIM = 64
SSD_HEADS = SSD_WIDTH // SSD_HEAD_DIM
SSD_GROUPS = 2
SSD_STATE = 128
SSD_CHUNK = 128
SSD_CONV_DIM = SSD_WIDTH + 2 * SSD_GROUPS * SSD_STATE

GDN_WIDTH = GROUP_WIDTH
GDN_HEAD_DIM = 128
GDN_HEADS = GDN_WIDTH // GDN_HEAD_DIM
GDN_CHUNK = 64
GDN_CONV_DIM = 3 * GDN_WIDTH

CONV_K = 5

FNET_WIDTH = GROUP_WIDTH
FNET_HEADS = 4
FNET_HEAD_DIM = FNET_WIDTH // FNET_HEADS

S5_WIDTH = GROUP_WIDTH
S5_GROUP_CH = 16
S5_GROUPS = S5_WIDTH // S5_GROUP_CH
S5_STATE = 64

SSD_IN = 2 * SSD_WIDTH + 2 * SSD_GROUPS * SSD_STATE + 2 * SSD_HEADS
GDN_IN = 4 * GDN_WIDTH + 4 * GDN_HEADS
IN_COLS = SSD_IN + GDN_IN + FNET_WIDTH + S5_WIDTH

N_EXPERTS = 8
TOP_K = 2
EPS = 1e-6

LANES = 128
VMEM_LIMIT_BYTES = 56 * 1024 * 1024

U_SSD_Z = 0
U_SSD_XBC = U_SSD_Z + SSD_WIDTH
U_GDN_QKV = U_SSD_XBC + SSD_CONV_DIM
U_GDN_Z = U_GDN_QKV + GDN_CONV_DIM
U_FNET = U_GDN_Z + GDN_WIDTH
U_S5 = U_FNET + FNET_WIDTH
U_WIDE = U_S5 + S5_WIDTH
NARROW = 2 * SSD_HEADS + 4 * GDN_HEADS

ROW_TILE = 1024
FF_TILE = 256


def _params(semantics):
    return pltpu.CompilerParams(dimension_semantics=semantics, vmem_limit_bytes=VMEM_LIMIT_BYTES)


def _rms_normalize(x, gain):
    ms = jnp.mean(x * x, axis=-1, keepdims=True)
    return x * lax.rsqrt(ms + EPS) * gain


def _norm_inproj_kernel(x_ref, g_ref, w_ref, wn_ref, o_ref, on_ref, h_ref):
    @pl.when(pl.program_id(1) == 0)
    def _():
        h = _rms_normalize(x_ref[...], g_ref[...]).astype(BF16)
        h_ref[...] = h
        on_ref[...] = jnp.dot(h, wn_ref[...], preferred_element_type=F32)

    o_ref[...] = jnp.dot(h_ref[...], w_ref[...], preferred_element_type=F32)


def norm_inproj(x, gain, w_wide, w_narrow, *, tm=ROW_TILE, tn=512):
    t, d = x.shape
    n = w_wide.shape[1]
    return pl.pallas_call(
        _norm_inproj_kernel,
        grid=(t // tm, n // tn),
        in_specs=[
            pl.BlockSpec((tm, d), lambda i, j: (i, 0)),
            pl.BlockSpec((1, d), lambda i, j: (0, 0)),
            pl.BlockSpec((d, tn), lambda i, j: (0, j)),
            pl.BlockSpec((d, LANES), lambda i, j: (0, 0)),
        ],
        out_specs=[
            pl.BlockSpec((tm, tn), lambda i, j: (i, j)),
            pl.BlockSpec((tm, LANES), lambda i, j: (i, 0)),
        ],
        out_shape=[jax.ShapeDtypeStruct((t, n), F32), jax.ShapeDtypeStruct((t, LANES), F32)],
        scratch_shapes=[pltpu.VMEM((tm, d), BF16)],
        compiler_params=_params(("parallel", "arbitrary")),
        name="norm_inproj",
    )(x, gain.reshape(1, d), w_wide, w_narrow)


def _outproj_kernel(y_ref, w_ref, x_ref, o_ref):
    o_ref[...] = x_ref[...] + jnp.dot(y_ref[...].astype(BF16), w_ref[...], preferred_element_type=F32)


def outproj_residual(y, w, x, *, tm=512):
    t, d = x.shape
    k = y.shape[1]
    return pl.pallas_call(
        _outproj_kernel,
        grid=(t // tm,),
        in_specs=[
            pl.BlockSpec((tm, k), lambda i: (i, 0)),
            pl.BlockSpec((k, d), lambda i: (0, 0)),
            pl.BlockSpec((tm, d), lambda i: (i, 0)),
        ],
        out_specs=pl.BlockSpec((tm, d), lambda i: (i, 0)),
        out_shape=jax.ShapeDtypeStruct((t, d), F32),
        compiler_params=_params(("parallel",)),
        name="outproj_residual",
    )(y, w, x)


def _swiglu_step(h, wg_ref, wu_ref, wd_ref, acc_ref):
    g = jnp.dot(h, wg_ref[0].astype(BF16), preferred_element_type=F32)
    u = jnp.dot(h, wu_ref[0].astype(BF16), preferred_element_type=F32)
    a = (g * jax.nn.sigmoid(g) * u).astype(BF16)
    acc_ref[...] += jnp.dot(a, wd_ref[0].astype(BF16), preferred_element_type=F32)


def _dense_ffn_kernel(x_ref, g_ref, wg_ref, wu_ref, wd_ref, o_ref, h_ref, acc_ref):
    f = pl.program_id(1)

    @pl.when(f == 0)
    def _():
        h_ref[...] = _rms_normalize(x_ref[...], g_ref[...]).astype(BF16)
        acc_ref[...] = jnp.zeros_like(acc_ref)

    _swiglu_step(h_ref[...], wg_ref, wu_ref, wd_ref, acc_ref)

    @pl.when(f == pl.num_programs(1) - 1)
    def _():
        o_ref[...] = x_ref[...] + acc_ref[...]


def dense_ffn(x, gain, wg, wu, wd, layer, *, tm=512, tf=512):
    t, d = x.shape
    ff = wg.shape[2]
    return pl.pallas_call(
        _dense_ffn_kernel,
        grid=(t // tm, ff // tf),
        in_specs=[
            pl.BlockSpec((tm, d), lambda i, f: (i, 0)),
            pl.BlockSpec((1, d), lambda i, f: (0, 0)),
            pl.BlockSpec((1, d, tf), lambda i, f: (layer, 0, f)),
            pl.BlockSpec((1, d, tf), lambda i, f: (layer, 0, f)),
            pl.BlockSpec((1, tf, d), lambda i, f: (layer, f, 0)),
        ],
        out_specs=pl.BlockSpec((tm, d), lambda i, f: (i, 0)),
        out_shape=jax.ShapeDtypeStruct((t, d), F32),
        scratch_shapes=[pltpu.VMEM((tm, d), BF16), pltpu.VMEM((tm, d), F32)],
        compiler_params=_params(("parallel", "arbitrary")),
        name="dense_ffn",
    )(x, gain.reshape(1, d), wg, wu, wd)


def _moe_ffn_kernel(te_ref, na_ref, x_ref, s_ref, wg_ref, wu_ref, wd_ref, o_ref, acc_ref):
    i = pl.program_id(0)
    f = pl.program_id(1)

    @pl.when(i < na_ref[0])
    def _():
        @pl.when(f == 0)
        def _():
            acc_ref[...] = jnp.zeros_like(acc_ref)

        _swiglu_step(x_ref[...], wg_ref, wu_ref, wd_ref, acc_ref)

        @pl.when(f == pl.num_programs(1) - 1)
        def _():
            o_ref[...] = acc_ref[...] * s_ref[...]


def moe_ffn_grouped(xs, scale, tile_expert, n_active, wg, wu, wd, layer, *, tm=ROW_TILE, tf=FF_TILE):
    p, d = xs.shape
    n_e, ff = wg.shape[1], wg.shape[3]
    n_f = ff // tf
    wg = wg.reshape(-1, d, ff)
    wu = wu.reshape(-1, d, ff)
    wd = wd.reshape(-1, ff, d)

    def row_idx(i, f, te, na):
        return (jnp.minimum(i, na[0] - 1), 0)

    def up_idx(i, f, te, na):
        live = i < na[0]
        return (layer * n_e + te[jnp.minimum(i, na[0] - 1)], 0, jnp.where(live, f, n_f - 1))

    def down_idx(i, f, te, na):
        live = i < na[0]
        return (layer * n_e + te[jnp.minimum(i, na[0] - 1)], jnp.where(live, f, n_f - 1), 0)

    grid_spec = pltpu.PrefetchScalarGridSpec(
        num_scalar_prefetch=2,
        grid=(p // tm, n_f),
        in_specs=[
            pl.BlockSpec((tm, d), row_idx),
            pl.BlockSpec((tm, 1), row_idx),
            pl.BlockSpec((1, d, tf), up_idx),
            pl.BlockSpec((1, d, tf), up_idx),
            pl.BlockSpec((1, tf, d), down_idx),
        ],
        out_specs=pl.BlockSpec((tm, d), row_idx),
        scratch_shapes=[pltpu.VMEM((tm, d), F32)],
    )
    return pl.pallas_call(
        _moe_ffn_kernel,
        grid_spec=grid_spec,
        out_shape=jax.ShapeDtypeStruct((p, d), F32),
        compiler_params=_params(("arbitrary", "arbitrary")),
        name="moe_ffn",
    )(tile_expert, n_active, xs, scale, wg, wu, wd)


def _norm_router_kernel(x_ref, g_ref, wr_ref, h_ref, l_ref):
    h = _rms_normalize(x_ref[...], g_ref[...])
    h_ref[...] = h.astype(BF16)
    l_ref[...] = jnp.dot(h, wr_ref[...], preferred_element_type=F32, precision=lax.Precision.HIGHEST)


def norm_router(x, gain, router_w_padded, *, tm=512):
    t, d = x.shape
    return pl.pallas_call(
        _norm_router_kernel,
        grid=(t // tm,),
        in_specs=[
            pl.BlockSpec((tm, d), lambda i: (i, 0)),
            pl.BlockSpec((1, d), lambda i: (0, 0)),
            pl.BlockSpec((d, LANES), lambda i: (0, 0)),
        ],
        out_specs=[pl.BlockSpec((tm, d), lambda i: (i, 0)), pl.BlockSpec((tm, LANES), lambda i: (i, 0))],
        out_shape=[jax.ShapeDtypeStruct((t, d), BF16), jax.ShapeDtypeStruct((t, LANES), F32)],
        compiler_params=_params(("parallel",)),
        name="norm_router",
    )(x, gain.reshape(1, d), router_w_padded)


def _final_norm_kernel(x_ref, g_ref, o_ref):
    o_ref[...] = _rms_normalize(x_ref[...], g_ref[...])


def final_norm(x, gain, *, tm=512):
    t, d = x.shape
    return pl.pallas_call(
        _final_norm_kernel,
        grid=(t // tm,),
        in_specs=[pl.BlockSpec((tm, d), lambda i: (i, 0)), pl.BlockSpec((1, d), lambda i: (0, 0))],
        out_specs=pl.BlockSpec((tm, d), lambda i: (i, 0)),
        out_shape=jax.ShapeDtypeStruct((t, d), F32),
        compiler_params=_params(("parallel",)),
        name="final_norm",
    )(x, gain.reshape(1, d))


def moe_block(x, gain, router_w, wg, wu, wd, layer, *, tm=ROW_TILE, tf=FF_TILE):
    t, d = x.shape
    n_e = router_w.shape[1]
    wr = jnp.zeros((d, LANES), F32).at[:, :n_e].set(router_w)
    h, logits = norm_router(x, gain, wr)
    top_logit, top_idx = lax.top_k(logits[:, :n_e], TOP_K)
    top_w = jax.nn.softmax(top_logit, axis=-1)

    n_assign = t * TOP_K
    n_tiles = (n_assign + n_e * (tm - 1)) // tm + 1
    flat_e = top_idx.reshape(-1)
    order = jnp.argsort(flat_e, stable=True)
    sorted_e = flat_e[order]
    counts = jnp.zeros((n_e,), jnp.int32).at[flat_e].add(1)
    padded = ((counts + tm - 1) // tm) * tm
    pad_end = jnp.cumsum(padded)
    pad_start = pad_end - padded
    start = jnp.cumsum(counts) - counts
    dest = pad_start[sorted_e] + jnp.arange(n_assign, dtype=jnp.int32) - start[sorted_e]
    src_token = jnp.zeros((n_tiles * tm,), jnp.int32).at[dest].set((order // TOP_K).astype(jnp.int32))
    scale = jnp.zeros((n_tiles * tm,), F32).at[dest].set(top_w.reshape(-1)[order])
    pos = jnp.zeros((n_assign,), jnp.int32).at[order].set(dest).reshape(t, TOP_K)
    tile_start = jnp.arange(n_tiles, dtype=jnp.int32) * tm
    tile_expert = jnp.minimum(jnp.searchsorted(pad_end, tile_start, side="right"), n_e - 1).astype(jnp.int32)
    n_active = (pad_end[-1] // tm).astype(jnp.int32).reshape(1)

    xs = jnp.take(h, src_token, axis=0)
    ys = moe_ffn_grouped(xs, scale.reshape(-1, 1), tile_expert, n_active, wg, wu, wd, layer, tm=tm, tf=tf)
    return x + jnp.take(ys, pos[:, 0], axis=0) + jnp.take(ys, pos[:, 1], axis=0)


def _rmsnorm(x, w):
    return x * lax.rsqrt(jnp.mean(jnp.square(x), axis=-1, keepdims=True) + EPS) * w


def _l2norm(t):
    return t * lax.rsqrt(jnp.sum(jnp.square(t), axis=-1, keepdims=True) + EPS)


def _flip(t):
    return jnp.flip(t, axis=1)


def _centred_dwconv(x, w):
    rhs = jnp.transpose(w)[:, None, :].astype(x.dtype)
    return lax.conv_general_dilated(x, rhs, window_strides=(1,), padding=[(CONV_K // 2, CONV_K // 2)],
                                    dimension_numbers=('NWC', 'WIO', 'NWC'), feature_group_count=x.shape[-1])


NARROW_GDN_A = 2 * SSD_HEADS
NARROW_GDN_B = NARROW_GDN_A + 2 * GDN_HEADS


def _mm(a, b):
    return jnp.dot(a.astype(BF16), b.astype(BF16), preferred_element_type=F32)


def _mm_nt(a, b):
    return lax.dot_general(a.astype(BF16), b.astype(BF16), (((1,), (1,)), ((), ())), preferred_element_type=F32)


def _mm_tn(a, b):
    return lax.dot_general(a.astype(BF16), b.astype(BF16), (((0,), (0,)), ((), ())), preferred_element_type=F32)


def _unit_triangular_inverses(lmats, row, col):
    q = lmats[0].shape[0]
    eye = (row == col).astype(F32)

    def same_block(shift):
        return (row >> shift) == (col >> shift)

    blk = same_block(3)
    ds = [jnp.where(blk, m, 0.0) for m in lmats]
    d2s = [_mm(d, d) for d in ds]
    d4s = [_mm(d2, d2) for d2 in d2s]
    xs = [_mm(eye - d, eye + d2) for d, d2 in zip(ds, d2s)]
    xs = [_mm(x, eye + d4) for x, d4 in zip(xs, d4s)]
    shift = 4
    while (1 << shift) <= q:
        level = same_block(shift) & jnp.logical_not(same_block(shift - 1))
        xes = [_mm(x, jnp.where(level, m, 0.0)) for x, m in zip(xs, lmats)]
        xs = [x - _mm(xe, x) for x, xe in zip(xs, xes)]
        shift += 1
    return xs


def _gdn_chunk_operands(q_ref, k_ref, v_ref, n_ref, ap_ref, bias_ref, direction):
    cq = q_ref.shape[0]
    row = lax.broadcasted_iota(jnp.int32, (cq, cq), 0)
    col = lax.broadcasted_iota(jnp.int32, (cq, cq), 1)
    if direction == 0:
        incl, strict = col <= row, col < row
    else:
        incl, strict = col >= row, col > row
    nar = n_ref[...]
    g_all = ap_ref[...] * jax.nn.softplus(nar + bias_ref[...])
    beta_all = jax.nn.sigmoid(nar)
    cum_op = jnp.concatenate([incl.astype(F32), jnp.ones((cq, cq), F32)], axis=0)
    cums = jnp.dot(cum_op, g_all, preferred_element_type=F32, precision=lax.Precision.HIGHEST)
    gc_all, tot_all = cums[:cq], cums[cq:]
    gc_t = gc_all.T
    egc_all = jnp.exp(gc_all)
    erem_all = jnp.exp(tot_all - gc_all)
    etot_all = jnp.exp(tot_all)
    heads = range(GDN_HEADS)
    la = [NARROW_GDN_A + direction * GDN_HEADS + h for h in heads]
    lb = [NARROW_GDN_B + direction * GDN_HEADS + h for h in heads]
    hs = [slice(h * GDN_HEAD_DIM, (h + 1) * GDN_HEAD_DIM) for h in heads]
    q = [q_ref[:, hs[h]] for h in heads]
    k = [k_ref[:, hs[h]] for h in heads]
    v = [v_ref[:, hs[h]] for h in heads]
    beta = [beta_all[:, lb[h]:lb[h] + 1] for h in heads]
    egc = [egc_all[:, la[h]:la[h] + 1] for h in heads]
    decay = [jnp.exp(jnp.where(incl, gc_all[:, la[h]:la[h] + 1] - gc_t[la[h]:la[h] + 1, :], -1e30)) for h in heads]
    kb = [k[h] * beta[h] for h in heads]
    return dict(
        q=q, k=k, decay=decay, strict=[strict] * GDN_HEADS,
        qe=[q[h] * egc[h] for h in heads],
        rhs=[jnp.concatenate([v[h] * beta[h], kb[h] * egc[h]], axis=-1) for h in heads],
        kb=kb,
        ke=[k[h] * erem_all[:, la[h]:la[h] + 1] for h in heads],
        etot=[etot_all[0:1, la[h]:la[h] + 1] for h in heads],
    )


def _gdn_scan_kernel(qf, kf, vf, nf, qb, kb, vb, nb, ap_ref, bias_ref, of_ref, ob_ref, s_ref):
    @pl.when(pl.program_id(1) == 0)
    def _():
        s_ref[...] = jnp.zeros_like(s_ref)

    cq = qf.shape[0]
    row = lax.broadcasted_iota(jnp.int32, (cq, cq), 0)
    col = lax.broadcasted_iota(jnp.int32, (cq, cq), 1)
    fwd = _gdn_chunk_operands(qf, kf, vf, nf, ap_ref, bias_ref, 0)
    bwd = _gdn_chunk_operands(qb, kb, vb, nb, ap_ref, bias_ref, 1)
    c = {name: fwd[name] + bwd[name] for name in fwd}
    n = range(2 * GDN_HEADS)
    s = [s_ref[i] for i in n]
    kk = [_mm_nt(c['kb'][i], c['k'][i]) for i in n]
    qk = [_mm_nt(c['q'][i], c['k'][i]) for i in n]
    lmats = [jnp.where(c['strict'][i], kk[i] * c['decay'][i], 0.0) for i in n]
    t_inv = _unit_triangular_inverses(lmats, row, col)
    uw = [_mm(t_inv[i], c['rhs'][i]) for i in n]
    ws = [_mm(uw[i][:, GDN_HEAD_DIM:], s[i]) for i in n]
    qs = [_mm(c['qe'][i], s[i]) for i in n]
    v_new = [uw[i][:, :GDN_HEAD_DIM] - ws[i] for i in n]
    outs = [qs[i] + _mm(qk[i] * c['decay'][i], v_new[i]) for i in n]
    upd = [_mm_tn(c['ke'][i], v_new[i]) for i in n]
    of_ref[...] = jnp.concatenate(outs[:GDN_HEADS], axis=-1)
    ob_ref[...] = jnp.concatenate(outs[GDN_HEADS:], axis=-1)
    for i in n:
        s_ref[i] = s[i] * c['etot'][i] + upd[i]


def gdn_scan(q, k, v, u_narrow, a_par, bias_par, batch, *, cq=GDN_CHUNK):
    t, w = q.shape
    nc = t // batch // cq

    def fwd(b, c):
        return (b * nc + c, 0)

    def bwd(b, c):
        return (b * nc + nc - 1 - c, 0)

    wide = lambda idx: pl.BlockSpec((cq, w), idx)
    narrow = lambda idx: pl.BlockSpec((cq, LANES), idx)
    par = pl.BlockSpec((1, LANES), lambda b, c: (0, 0))
    return pl.pallas_call(
        _gdn_scan_kernel,
        grid=(batch, nc),
        in_specs=[wide(fwd), wide(fwd), wide(fwd), narrow(fwd), wide(bwd), wide(bwd), wide(bwd), narrow(bwd), par, par],
        out_specs=[wide(fwd), wide(bwd)],
        out_shape=[jax.ShapeDtypeStruct((t, w), F32), jax.ShapeDtypeStruct((t, w), F32)],
        scratch_shapes=[pltpu.VMEM((2 * GDN_HEADS, GDN_HEAD_DIM, GDN_HEAD_DIM), F32)],
        compiler_params=_params(("arbitrary", "arbitrary")),
        name="gdn_scan",
    )(q, k, v, u_narrow, q, k, v, u_narrow, a_par, bias_par)


def _gdn_gate_kernel(of_ref, ob_ref, z_ref, w_ref, o_ref):
    for h in range(GDN_HEADS):
        hs = slice(h * GDN_HEAD_DIM, (h + 1) * GDN_HEAD_DIM)
        o = _rms_normalize(of_ref[:, hs] + ob_ref[:, hs], w_ref[...])
        z = z_ref[:, hs]
        o_ref[:, hs] = o * (z * jax.nn.sigmoid(z))


def gdn_gate(o_f, o_b, u_wide, norm_w, *, tm=512):
    t, w = o_f.shape
    zblk = U_GDN_Z // GDN_WIDTH
    row = lambda i: (i, 0)
    return pl.pallas_call(
        _gdn_gate_kernel,
        grid=(t // tm,),
        in_specs=[pl.BlockSpec((tm, w), row), pl.BlockSpec((tm, w), row), pl.BlockSpec((tm, w), lambda i: (i, zblk)),
                  pl.BlockSpec((1, GDN_HEAD_DIM), lambda i: (0, 0))],
        out_specs=pl.BlockSpec((tm, w), row),
        out_shape=jax.ShapeDtypeStruct((t, w), F32),
        compiler_params=_params(("parallel",)),
        name="gdn_gate",
    )(o_f, o_b, u_wide, norm_w.reshape(1, -1))


def gdn_mixer(u_wide, u_narrow, batch, conv_w, dt_bias, a_log, norm_w):
    t = u_wide.shape[0]
    L = t // batch
    qkv = u_wide[:, U_GDN_QKV:U_GDN_Z].reshape(batch, L, GDN_CONV_DIM)
    qkv = jax.nn.silu(_centred_dwconv(qkv, conv_w))
    heads = lambda x: x.reshape(batch, L, GDN_HEADS, GDN_HEAD_DIM)
    q = (_l2norm(heads(qkv[..., :GDN_WIDTH])) * (GDN_HEAD_DIM ** -0.5)).reshape(t, GDN_WIDTH)
    k = _l2norm(heads(qkv[..., GDN_WIDTH:2 * GDN_WIDTH])).reshape(t, GDN_WIDTH)
    v = qkv[..., 2 * GDN_WIDTH:].reshape(t, GDN_WIDTH)
    lanes = slice(NARROW_GDN_A, NARROW_GDN_B)
    a_par = jnp.zeros((1, LANES), F32).at[0, lanes].set(-jnp.exp(a_log).reshape(-1))
    bias_par = jnp.zeros((1, LANES), F32).at[0, lanes].set(dt_bias.reshape(-1))
    o_f, o_b = gdn_scan(q, k, v, u_narrow, a_par, bias_par, batch)
    return gdn_gate(o_f, o_b, u_wide, norm_w)


SSD_BC = SSD_GROUPS * SSD_STATE
SSD_REP = SSD_HEADS // SSD_GROUPS


def _ssd_chunk_operands(xbc_ref, n_ref, ap_ref, bias_ref, direction):
    cq = xbc_ref.shape[0]
    row = lax.broadcasted_iota(jnp.int32, (cq, cq), 0)
    col = lax.broadcasted_iota(jnp.int32, (cq, cq), 1)
    incl = col <= row if direction == 0 else col >= row
    dt_all = jax.nn.softplus(n_ref[...] + bias_ref[...])
    cum_op = jnp.concatenate([incl.astype(F32), jnp.ones((cq, cq), F32)], axis=0)
    cums = jnp.dot(cum_op, dt_all * ap_ref[...], preferred_element_type=F32, precision=lax.Precision.HIGHEST)
    ac_all, tot_all = cums[:cq], cums[cq:]
    ac_t = ac_all.T
    eac_all = jnp.exp(ac_all)
    erem_all = jnp.exp(tot_all - ac_all)
    etot_all = jnp.exp(tot_all)
    groups = range(SSD_GROUPS)
    bm = [xbc_ref[:, SSD_WIDTH + g * SSD_STATE:SSD_WIDTH + (g + 1) * SSD_STATE] for g in groups]
    cm = [xbc_ref[:, SSD_WIDTH + SSD_BC + g * SSD_STATE:SSD_WIDTH + SSD_BC + (g + 1) * SSD_STATE] for g in groups]
    cb = [_mm_nt(cm[g], bm[g]) for g in groups]
    heads = range(SSD_HEADS)
    ln = [direction * SSD_HEADS + h for h in heads]
    seg = [jnp.exp(jnp.where(incl, ac_all[:, ln[h]:ln[h] + 1] - ac_t[ln[h]:ln[h] + 1, :], -1e30)) for h in heads]
    return dict(
        scores=[cb[h // SSD_REP] * seg[h] for h in heads],
        xdt=[xbc_ref[:, h * SSD_HEAD_DIM:(h + 1) * SSD_HEAD_DIM] * dt_all[:, ln[h]:ln[h] + 1] for h in heads],
        b_end=[bm[h // SSD_REP] * erem_all[:, ln[h]:ln[h] + 1] for h in heads],
        c_in=[cm[h // SSD_REP] * eac_all[:, ln[h]:ln[h] + 1] for h in heads],
        etot=[etot_all[0:1, ln[h]:ln[h] + 1] for h in heads],
    )


def _ssd_scan_kernel(xf, nf, xb, nb, ap_ref, bias_ref, of_ref, ob_ref, s_ref):
    @pl.when(pl.program_id(1) == 0)
    def _():
        s_ref[...] = jnp.zeros_like(s_ref)

    fwd = _ssd_chunk_operands(xf, nf, ap_ref, bias_ref, 0)
    bwd = _ssd_chunk_operands(xb, nb, ap_ref, bias_ref, 1)
    c = {name: fwd[name] + bwd[name] for name in fwd}
    n = range(2 * SSD_HEADS)
    s = [s_ref[i] for i in n]
    y = [_mm(c['scores'][i], c['xdt'][i]) + _mm(c['c_in'][i], s[i]) for i in n]
    upd = [_mm_tn(c['b_end'][i], c['xdt'][i]) for i in n]
    of_ref[...] = jnp.concatenate(y[:SSD_HEADS], axis=-1)
    ob_ref[...] = jnp.concatenate(y[SSD_HEADS:], axis=-1)
    for i in n:
        s_ref[i] = s[i] * c['etot'][i] + upd[i]


def ssd_scan(xbc, u_narrow, a_par, bias_par, batch, *, cq=SSD_CHUNK):
    t = xbc.shape[0]
    nc = t // batch // cq

    def fwd(b, c):
        return (b * nc + c, 0)

    def bwd(b, c):
        return (b * nc + nc - 1 - c, 0)

    wide = lambda idx: pl.BlockSpec((cq, SSD_CONV_DIM), idx)
    narrow = lambda idx: pl.BlockSpec((cq, LANES), idx)
    out = lambda idx: pl.BlockSpec((cq, SSD_WIDTH), idx)
    par = pl.BlockSpec((1, LANES), lambda b, c: (0, 0))
    return pl.pallas_call(
        _ssd_scan_kernel,
        grid=(batch, nc),
        in_specs=[wide(fwd), narrow(fwd), wide(bwd), narrow(bwd), par, par],
        out_specs=[out(fwd), out(bwd)],
        out_shape=[jax.ShapeDtypeStruct((t, SSD_WIDTH), F32), jax.ShapeDtypeStruct((t, SSD_WIDTH), F32)],
        scratch_shapes=[pltpu.VMEM((2 * SSD_HEADS, SSD_STATE, SSD_HEAD_DIM), F32)],
        compiler_params=_params(("arbitrary", "arbitrary")),
        name="ssd_scan",
    )(xbc, u_narrow, xbc, u_narrow, a_par, bias_par)


def _ssd_gate_kernel(yf_ref, yb_ref, x_ref, z_ref, d_ref, w_ref, o_ref):
    z = z_ref[...]
    y = (yf_ref[...] + yb_ref[...] + x_ref[...] * d_ref[...]) * (z * jax.nn.sigmoid(z))
    o_ref[...] = _rms_normalize(y, w_ref[...])


def ssd_gate(y_f, y_b, xbc, u_wide, d_wide, norm_w, *, tm=512):
    t, w = y_f.shape
    row = lambda i: (i, 0)
    vec = pl.BlockSpec((1, w), lambda i: (0, 0))
    return pl.pallas_call(
        _ssd_gate_kernel,
        grid=(t // tm,),
        in_specs=[pl.BlockSpec((tm, w), row), pl.BlockSpec((tm, w), row), pl.BlockSpec((tm, w), row),
                  pl.BlockSpec((tm, w), lambda i: (i, U_SSD_Z // SSD_WIDTH)), vec, vec],
        out_specs=pl.BlockSpec((tm, w), row),
        out_shape=jax.ShapeDtypeStruct((t, w), F32),
        compiler_params=_params(("parallel",)),
        name="ssd_gate",
    )(y_f, y_b, xbc, u_wide, d_wide, norm_w.reshape(1, -1))


def ssd_mixer(u_wide, u_narrow, batch, conv_w, conv_b, dt_bias, a_log, d_skip, norm_w):
    t = u_wide.shape[0]
    L = t // batch
    xbc = u_wide[:, U_SSD_XBC:U_GDN_QKV].reshape(batch, L, SSD_CONV_DIM)
    xbc = jax.nn.silu(_centred_dwconv(xbc, conv_w) + conv_b).reshape(t, SSD_CONV_DIM)
    lanes = slice(0, 2 * SSD_HEADS)
    a_par = jnp.zeros((1, LANES), F32).at[0, lanes].set(-jnp.exp(a_log).reshape(-1))
    bias_par = jnp.zeros((1, LANES), F32).at[0, lanes].set(dt_bias.reshape(-1))
    y_f, y_b = ssd_scan(xbc, u_narrow, a_par, bias_par, batch)
    d_wide = jnp.repeat(d_skip, SSD_HEAD_DIM).reshape(1, SSD_WIDTH)
    return ssd_gate(y_f, y_b, xbc, u_wide, d_wide, norm_w)


def fourier_mixer(u, w, b):
    Bn, L, _ = u.shape
    uh = u.reshape(Bn, L, FNET_HEADS, FNET_HEAD_DIM)
    f = jnp.fft.fft2(uh, axes=(1, 3), norm='ortho').real
    return f.reshape(Bn, L, FNET_WIDTH) @ w + b


S5_TILE = 256
S5_NSTATE = S5_GROUPS * S5_STATE
S5_COLS = 512
S5_BLOCKS = S5_WIDTH // LANES


def _s5_scan_kernel(u_ref, pi_ref, pit_ref, b_ref, c_ref, lre_ref, lim_ref, o_ref,
                    sre, sim, pre, pim, car_re, car_im, e_re, e_im, cin_re, cin_im):
    tl = u_ref.shape[0]
    steps = tl // 8
    first_tile = pl.program_id(2) == 0

    def col(j):
        return slice(j * S5_COLS, (j + 1) * S5_COLS)

    def rows(t):
        return pl.ds(pl.multiple_of(t * 8, 8), 8)

    def bcast(ref, j):
        return jnp.broadcast_to(ref[0, :, col(j)], (8, S5_COLS))

    @pl.when(first_tile)
    def _():
        car_re[...] = jnp.zeros_like(car_re)
        car_im[...] = jnp.zeros_like(car_im)
        for j in range(S5_BLOCKS):
            lr, li = bcast(lre_ref, j), bcast(lim_ref, j)

            def power_step(t, p, j=j, lr=lr, li=li):
                pr, pi_ = p
                pre[rows(t), col(j)] = pr
                pim[rows(t), col(j)] = pi_
                return pr * lr - pi_ * li, pr * li + pi_ * lr

            lax.fori_loop(0, steps, power_step, (lr, li))

    up = jnp.dot(pi_ref[0], u_ref[...].astype(BF16), preferred_element_type=F32).astype(BF16)
    for j in range(S5_BLOCKS):
        uj = up[:, j * LANES:(j + 1) * LANES]
        sre[:, col(j)] = jnp.dot(uj, b_ref[0, j, 0], preferred_element_type=F32)
        sim[:, col(j)] = jnp.dot(uj, b_ref[0, j, 1], preferred_element_type=F32)

    for j in range(S5_BLOCKS):
        lr, li = bcast(lre_ref, j), bcast(lim_ref, j)

        def scan_step(t, s, j=j, lr=lr, li=li):
            sr, si = s
            nr = lr * sr - li * si + sre[rows(t), col(j)]
            ni = lr * si + li * sr + sim[rows(t), col(j)]
            sre[rows(t), col(j)] = nr
            sim[rows(t), col(j)] = ni
            return nr, ni

        zero = jnp.zeros((8, S5_COLS), F32)
        er, ei = lax.fori_loop(0, steps, scan_step, (zero, zero), unroll=4)
        e_re[:, col(j)] = er
        e_im[:, col(j)] = ei

    last = pl.ds((steps - 1) * 8, 1)
    lsr, lsi = pre[last, :], pim[last, :]
    cr, ci = car_re[...], car_im[...]
    for r in range(8):
        cin_re[r:r + 1, :] = cr
        cin_im[r:r + 1, :] = ci
        cr, ci = (e_re[r:r + 1, :] + lsr * cr - lsi * ci, e_im[r:r + 1, :] + lsr * ci + lsi * cr)
    car_re[...] = cr
    car_im[...] = ci

    for j in range(S5_BLOCKS):
        cr, ci = cin_re[:, col(j)], cin_im[:, col(j)]

        def fix_step(t, carry, j=j, cr=cr, ci=ci):
            pr, pi_ = pre[rows(t), col(j)], pim[rows(t), col(j)]
            sre[rows(t), col(j)] = sre[rows(t), col(j)] + pr * cr - pi_ * ci
            sim[rows(t), col(j)] = sim[rows(t), col(j)] + pr * ci + pi_ * cr
            return carry

        lax.fori_loop(0, steps, fix_step, 0, unroll=4)

    ys = []
    for j in range(S5_BLOCKS):
        ys.append(jnp.dot(sre[:, col(j)].astype(BF16), c_ref[0, j, 0], preferred_element_type=F32)
                  + jnp.dot(sim[:, col(j)].astype(BF16), c_ref[0, j, 1], preferred_element_type=F32))
    y = jnp.concatenate(ys, axis=-1)
    y_hi = y.astype(BF16)
    y_lo = (y - y_hi.astype(F32)).astype(BF16)
    pit = pit_ref[0]
    o_ref[0] = jnp.dot(pit, y_hi, preferred_element_type=F32) + jnp.dot(pit, y_lo, preferred_element_type=F32)


def _s5_operators(lam_re, lam_im, log_step, b_re, b_im, c_re, c_im, tl):
    lam = lax.complex(lam_re, lam_im)
    delta = jnp.exp(log_step)[..., None]
    lam_bar = jnp.exp(lam * delta)
    b_scale = (lam_bar - 1.0) / lam
    eye = jnp.eye(8, dtype=F32)
    gpb = S5_GROUPS // S5_BLOCKS
    bd = lax.complex(b_re, b_im)[None] * b_scale[..., None]
    bd = bd.reshape(2, S5_BLOCKS, gpb, S5_STATE, S5_GROUP_CH)
    bmat = jnp.stack([jnp.einsum('djgpc,gh->djgchp', part, eye).reshape(2, S5_BLOCKS, LANES, S5_COLS)
                      for part in (bd.real, bd.imag)], axis=2).astype(BF16)
    cd = lax.complex(c_re, c_im).reshape(2, S5_BLOCKS, gpb, S5_GROUP_CH, S5_STATE)
    cmat = jnp.stack([jnp.einsum('djgcp,gh->djhpgc', part, eye).reshape(2, S5_BLOCKS, S5_COLS, LANES)
                      for part in (cd.real, -cd.imag)], axis=2).astype(BF16)
    lre = lam_bar.real.reshape(2, 1, S5_NSTATE)
    lim = lam_bar.imag.reshape(2, 1, S5_NSTATE)
    steps = tl // 8
    rt = jnp.arange(tl)
    src = (rt % 8) * steps + rt // 8
    fwd = (src[:, None] == rt[None, :])
    bwd = ((tl - 1 - src)[:, None] == rt[None, :])
    pi = jnp.stack([fwd, bwd]).astype(BF16)
    return pi, jnp.swapaxes(pi, 1, 2), bmat, cmat, lre, lim


def s5_scan(u_wide, ops, batch, *, tl=S5_TILE):
    pi, pit, bmat, cmat, lre, lim = ops
    t = u_wide.shape[0]
    nt = t // batch // tl
    ublk = U_S5 // S5_WIDTH

    def tile_idx(b, d, i):
        return b * nt + jnp.where(d == 0, i, nt - 1 - i)

    state = pltpu.VMEM((tl, S5_NSTATE), F32)
    row = pltpu.VMEM((1, S5_NSTATE), F32)
    seg = pltpu.VMEM((8, S5_NSTATE), F32)
    return pl.pallas_call(
        _s5_scan_kernel,
        grid=(batch, 2, nt),
        in_specs=[
            pl.BlockSpec((tl, S5_WIDTH), lambda b, d, i: (tile_idx(b, d, i), ublk)),
            pl.BlockSpec((1, tl, tl), lambda b, d, i: (d, 0, 0)),
            pl.BlockSpec((1, tl, tl), lambda b, d, i: (d, 0, 0)),
            pl.BlockSpec((1, S5_BLOCKS, 2, LANES, S5_COLS), lambda b, d, i: (d, 0, 0, 0, 0)),
            pl.BlockSpec((1, S5_BLOCKS, 2, S5_COLS, LANES), lambda b, d, i: (d, 0, 0, 0, 0)),
            pl.BlockSpec((1, 1, S5_NSTATE), lambda b, d, i: (d, 0, 0)),
            pl.BlockSpec((1, 1, S5_NSTATE), lambda b, d, i: (d, 0, 0)),
        ],
        out_specs=pl.BlockSpec((1, tl, S5_WIDTH), lambda b, d, i: (d, tile_idx(b, d, i), 0)),
        out_shape=jax.ShapeDtypeStruct((2, t, S5_WIDTH), F32),
        scratch_shapes=[state, state, state, state, row, row, seg, seg, seg, seg],
        compiler_params=_params(("arbitrary", "arbitrary", "arbitrary")),
        name="s5_scan",
    )(u_wide, pi, pit, bmat, cmat, lre, lim)


def _s5_gate_kernel(y_ref, u_ref, d_ref, w_ref, b_ref, o_ref):
    y = y_ref[0] + y_ref[1] + u_ref[...] * d_ref[...]
    h = jax.nn.gelu(y)
    gate = jnp.dot(h.astype(BF16), w_ref[...], preferred_element_type=F32) + b_ref[...]
    o_ref[...] = h * jax.nn.sigmoid(gate)


def s5_gate(y2, u_wide, d_skip, glu_w, glu_b, *, tm=512):
    t = u_wide.shape[0]
    ublk = U_S5 // S5_WIDTH
    return pl.pallas_call(
        _s5_gate_kernel,
        grid=(t // tm,),
        in_specs=[
            pl.BlockSpec((2, tm, S5_WIDTH), lambda i: (0, i, 0)),
            pl.BlockSpec((tm, S5_WIDTH), lambda i: (i, ublk)),
            pl.BlockSpec((1, S5_WIDTH), lambda i: (0, 0)),
            pl.BlockSpec((S5_WIDTH, S5_WIDTH), lambda i: (0, 0)),
            pl.BlockSpec((1, S5_WIDTH), lambda i: (0, 0)),
        ],
        out_specs=pl.BlockSpec((tm, S5_WIDTH), lambda i: (i, 0)),
        out_shape=jax.ShapeDtypeStruct((t, S5_WIDTH), F32),
        compiler_params=_params(("parallel",)),
        name="s5_gate",
    )(y2, u_wide, d_skip.reshape(1, -1), glu_w.astype(BF16), glu_b.reshape(1, -1))


def s5_mixer(u_wide, batch, ops, d_skip, glu_w, glu_b):
    return s5_gate(s5_scan(u_wide, ops, batch), u_wide, d_skip, glu_w, glu_b)


def mixers(u_wide, u_narrow, batch, p, i):
    t = u_wide.shape[0]
    L = t // batch
    uw = u_wide.reshape(batch, L, U_WIDE)
    y_ssd = ssd_mixer(u_wide, u_narrow, batch, p['ssd_conv_w'][i], p['ssd_conv_b'][i], p['ssd_dt_bias'][i],
                      p['ssd_a_log'][i], p['ssd_d'][i], p['ssd_norm'][i])
    y_gdn = gdn_mixer(u_wide, u_narrow, batch, p['gdn_conv_w'][i], p['gdn_dt_bias'][i], p['gdn_a_log'][i],
                      p['gdn_norm'][i])
    y_fnet = fourier_mixer(uw[..., U_FNET:U_S5], p['fnet_w'][i], p['fnet_b'][i])
    s5_ops = _s5_operators(p['s5_lambda_re'][i], p['s5_lambda_im'][i], p['s5_log_step'][i], p['s5_b_re'][i],
                           p['s5_b_im'][i], p['s5_c_re'][i], p['s5_c_im'][i], S5_TILE)
    y_s5 = s5_mixer(u_wide, batch, s5_ops, p['s5_d'][i], p['s5_glu_w'][i], p['s5_glu_b'][i])
    return jnp.concatenate([y_ssd.reshape(t, GROUP_WIDTH), y_gdn, y_fnet.reshape(t, GROUP_WIDTH), y_s5], axis=-1)


def _split_w_in(w_in):
    o1 = SSD_IN
    o2 = o1 + GDN_IN
    ssd_z = w_in[..., :SSD_WIDTH]
    ssd_xbc = w_in[..., SSD_WIDTH:SSD_WIDTH + SSD_CONV_DIM]
    ssd_dt = w_in[..., SSD_WIDTH + SSD_CONV_DIM:o1]
    gdn_qkv = w_in[..., o1:o1 + GDN_CONV_DIM]
    gdn_z = w_in[..., o1 + GDN_CONV_DIM:o1 + 4 * GDN_WIDTH]
    gdn_ab = w_in[..., o1 + 4 * GDN_WIDTH:o2]
    rest = w_in[..., o2:]
    wide = jnp.concatenate([ssd_z, ssd_xbc, gdn_qkv, gdn_z, rest], axis=-1).astype(BF16)
    pad = jnp.zeros(w_in.shape[:-1] + (LANES - NARROW,), w_in.dtype)
    narrow = jnp.concatenate([ssd_dt, gdn_ab, pad], axis=-1).astype(BF16)
    return wide, narrow


def kernel(x_prompt, x_sample, norm_mix, w_in, ssd_conv_w, ssd_conv_b, ssd_dt_bias, ssd_a_log, ssd_d, ssd_norm,
           gdn_conv_w, gdn_dt_bias, gdn_a_log, gdn_norm, fnet_w, fnet_b, s5_lambda_re, s5_lambda_im, s5_log_step,
           s5_b_re, s5_b_im, s5_c_re, s5_c_im, s5_d, s5_glu_w, s5_glu_b, w_out, norm_ffn, ffn_w_gate, ffn_w_up,
           ffn_w_down, router_w, moe_w_gate, moe_w_up, moe_w_down, norm_final):
    p = dict(ssd_conv_w=ssd_conv_w, ssd_conv_b=ssd_conv_b, ssd_dt_bias=ssd_dt_bias, ssd_a_log=ssd_a_log, ssd_d=ssd_d,
             ssd_norm=ssd_norm, gdn_conv_w=gdn_conv_w, gdn_dt_bias=gdn_dt_bias, gdn_a_log=gdn_a_log,
             gdn_norm=gdn_norm, fnet_w=fnet_w, fnet_b=fnet_b, s5_lambda_re=s5_lambda_re, s5_lambda_im=s5_lambda_im,
             s5_log_step=s5_log_step, s5_b_re=s5_b_re, s5_b_im=s5_b_im, s5_c_re=s5_c_re, s5_c_im=s5_c_im, s5_d=s5_d,
             s5_glu_w=s5_glu_w, s5_glu_b=s5_glu_b)
    bp, lp, d = x_prompt.shape
    bs, ls, _ = x_sample.shape
    tp = bp * lp
    x = jnp.concatenate([x_prompt.reshape(tp, d), x_sample.reshape(bs * ls, d)], axis=0)
    w_wide, w_narrow = _split_w_in(w_in)
    w_out_b = w_out.astype(BF16)
    ffn_w_gate, ffn_w_up, ffn_w_down = (w.astype(BF16) for w in (ffn_w_gate, ffn_w_up, ffn_w_down))
    depth = w_in.shape[0]
    for i in range(depth):
        u_wide, u_narrow = norm_inproj(x, norm_mix[i], w_wide[i], w_narrow[i])
        y = jnp.concatenate([mixers(u_wide[:tp], u_narrow[:tp], bp, p, i),
                             mixers(u_wide[tp:], u_narrow[tp:], bs, p, i)], axis=0)
        x = outproj_residual(y, w_out_b[i], x)
        j = i // 2
        if i % 2 == 0:
            x = dense_ffn(x, norm_ffn[i], ffn_w_gate, ffn_w_up, ffn_w_down, j)
        else:
            x = moe_block(x, norm_ffn[i], router_w[j], moe_w_gate, moe_w_up, moe_w_down, j)
    y = final_norm(x, norm_final)
    return y[:tp].reshape(bp, lp, d), y[tp:].reshape(bs, ls, d)
```

```python
import functools
import math

import jax
import jax.numpy as jnp
from jax import lax
from jax.experimental import pallas as pl
from jax.experimental.pallas import tpu as pltpu

F32 = jnp.float32
BF16 = jnp.bfloat16

D_MODEL = 2048
DEPTH = 4
GROUP_WIDTH = 512

SSD_WIDTH = GROUP_WIDTH
SSD_HEAD_DIM = 64
SSD_HEADS = SSD_WIDTH // SSD_HEAD_DIM
SSD_GROUPS = 2
SSD_STATE = 128
SSD_CHUNK = 128
SSD_CONV_DIM = SSD_WIDTH + 2 * SSD_GROUPS * SSD_STATE

GDN_WIDTH = GROUP_WIDTH
GDN_HEAD_DIM = 128
GDN_HEADS = GDN_WIDTH // GDN_HEAD_DIM
GDN_CHUNK = 64
GDN_CONV_DIM = 3 * GDN_WIDTH

CONV_K = 5

FNET_WIDTH = GROUP_WIDTH
FNET_HEADS = 4
FNET_HEAD_DIM = FNET_WIDTH // FNET_HEADS

S5_WIDTH = GROUP_WIDTH
S5_GROUP_CH = 16
S5_GROUPS = S5_WIDTH // S5_GROUP_CH
S5_STATE = 64

SSD_IN = 2 * SSD_WIDTH + 2 * SSD_GROUPS * SSD_STATE + 2 * SSD_HEADS
GDN_IN = 4 * GDN_WIDTH + 4 * GDN_HEADS
IN_COLS = SSD_IN + GDN_IN + FNET_WIDTH + S5_WIDTH

N_EXPERTS = 8
TOP_K = 2
EPS = 1e-6

LANES = 128
VMEM_LIMIT_BYTES = 56 * 1024 * 1024

U_SSD_Z = 0
U_SSD_XBC = U_SSD_Z + SSD_WIDTH
U_GDN_QKV = U_SSD_XBC + SSD_CONV_DIM
U_GDN_Z = U_GDN_QKV + GDN_CONV_DIM
U_FNET = U_GDN_Z + GDN_WIDTH
U_S5 = U_FNET + FNET_WIDTH
U_WIDE = U_S5 + S5_WIDTH
NARROW = 2 * SSD_HEADS + 4 * GDN_HEADS

ROW_TILE = 1024
FF_TILE = 256


def _params(semantics):
    return pltpu.CompilerParams(dimension_semantics=semantics, vmem_limit_bytes=VMEM_LIMIT_BYTES)


def _rms_normalize(x, gain):
    ms = jnp.mean(x * x, axis=-1, keepdims=True)
    return x * lax.rsqrt(ms + EPS) * gain


def _norm_inproj_kernel(x_ref, g_ref, w_ref, wn_ref, o_ref, on_ref, h_ref):
    @pl.when(pl.program_id(1) == 0)
    def _():
        h = _rms_normalize(x_ref[...], g_ref[...]).astype(BF16)
        h_ref[...] = h
        on_ref[...] = jnp.dot(h, wn_ref[...], preferred_element_type=F32)

    o_ref[...] = jnp.dot(h_ref[...], w_ref[...], preferred_element_type=F32)


def norm_inproj(x, gain, w_wide, w_narrow, *, tm=ROW_TILE, tn=512):
    t, d = x.shape
    n = w_wide.shape[1]
    return pl.pallas_call(
        _norm_inproj_kernel,
        grid=(t // tm, n // tn),
        in_specs=[
            pl.BlockSpec((tm, d), lambda i, j: (i, 0)),
            pl.BlockSpec((1, d), lambda i, j: (0, 0)),
            pl.BlockSpec((d, tn), lambda i, j: (0, j)),
            pl.BlockSpec((d, LANES), lambda i, j: (0, 0)),
        ],
        out_specs=[
            pl.BlockSpec((tm, tn), lambda i, j: (i, j)),
            pl.BlockSpec((tm, LANES), lambda i, j: (i, 0)),
        ],
        out_shape=[jax.ShapeDtypeStruct((t, n), F32), jax.ShapeDtypeStruct((t, LANES), F32)],
        scratch_shapes=[pltpu.VMEM((tm, d), BF16)],
        compiler_params=_params(("parallel", "arbitrary")),
        name="norm_inproj",
    )(x, gain.reshape(1, d), w_wide, w_narrow)


def _outproj_kernel(y_ref, w_ref, x_ref, o_ref):
    o_ref[...] = x_ref[...] + jnp.dot(y_ref[...].astype(BF16), w_ref[...], preferred_element_type=F32)


def outproj_residual(y, w, x, *, tm=512):
    t, d = x.shape
    k = y.shape[1]
    return pl.pallas_call(
        _outproj_kernel,
        grid=(t // tm,),
        in_specs=[
            pl.BlockSpec((tm, k), lambda i: (i, 0)),
            pl.BlockSpec((k, d), lambda i: (0, 0)),
            pl.BlockSpec((tm, d), lambda i: (i, 0)),
        ],
        out_specs=pl.BlockSpec((tm, d), lambda i: (i, 0)),
        out_shape=jax.ShapeDtypeStruct((t, d), F32),
        compiler_params=_params(("parallel",)),
        name="outproj_residual",
    )(y, w, x)


def _swiglu_step(h, wg_ref, wu_ref, wd_ref, acc_ref):
    g = jnp.dot(h, wg_ref[0].astype(BF16), preferred_element_type=F32)
    u = jnp.dot(h, wu_ref[0].astype(BF16), preferred_element_type=F32)
    a = (g * jax.nn.sigmoid(g) * u).astype(BF16)
    acc_ref[...] += jnp.dot(a, wd_ref[0].astype(BF16), preferred_element_type=F32)


def _dense_ffn_kernel(x_ref, g_ref, wg_ref, wu_ref, wd_ref, o_ref, h_ref, acc_ref):
    f = pl.program_id(1)

    @pl.when(f == 0)
    def _():
        h_ref[...] = _rms_normalize(x_ref[...], g_ref[...]).astype(BF16)
        acc_ref[...] = jnp.zeros_like(acc_ref)

    _swiglu_step(h_ref[...], wg_ref, wu_ref, wd_ref, acc_ref)

    @pl.when(f == pl.num_programs(1) - 1)
    def _():
        o_ref[...] = x_ref[...] + acc_ref[...]


def dense_ffn(x, gain, wg, wu, wd, layer, *, tm=512, tf=512):
    t, d = x.shape
    ff = wg.shape[2]
    return pl.pallas_call(
        _dense_ffn_kernel,
        grid=(t // tm, ff // tf),
        in_specs=[
            pl.BlockSpec((tm, d), lambda i, f: (i, 0)),
            pl.BlockSpec((1, d), lambda i, f: (0, 0)),
            pl.BlockSpec((1, d, tf), lambda i, f: (layer, 0, f)),
            pl.BlockSpec((1, d, tf), lambda i, f: (layer, 0, f)),
            pl.BlockSpec((1, tf, d), lambda i, f: (layer, f, 0)),
        ],
        out_specs=pl.BlockSpec((tm, d), lambda i, f: (i, 0)),
        out_shape=jax.ShapeDtypeStruct((t, d), F32),
        scratch_shapes=[pltpu.VMEM((tm, d), BF16), pltpu.VMEM((tm, d), F32)],
        compiler_params=_params(("parallel", "arbitrary")),
        name="dense_ffn",
    )(x, gain.reshape(1, d), wg, wu, wd)


def _moe_ffn_kernel(te_ref, na_ref, x_ref, s_ref, wg_ref, wu_ref, wd_ref, o_ref, acc_ref):
    i = pl.program_id(0)
    f = pl.program_id(1)

    @pl.when(i < na_ref[0])
    def _():
        @pl.when(f == 0)
        def _():
            acc_ref[...] = jnp.zeros_like(acc_ref)

        _swiglu_step(x_ref[...], wg_ref, wu_ref, wd_ref, acc_ref)

        @pl.when(f == pl.num_programs(1) - 1)
        def _():
            o_ref[...] = acc_ref[...] * s_ref[...]


def moe_ffn_grouped(xs, scale, tile_expert, n_active, wg, wu, wd, layer, *, tm=ROW_TILE, tf=FF_TILE):
    p, d = xs.shape
    n_e, ff = wg.shape[1], wg.shape[3]
    n_f = ff // tf
    wg = wg.reshape(-1, d, ff)
    wu = wu.reshape(-1, d, ff)
    wd = wd.reshape(-1, ff, d)

    def row_idx(i, f, te, na):
        return (jnp.minimum(i, na[0] - 1), 0)

    def up_idx(i, f, te, na):
        live = i < na[0]
        return (layer * n_e + te[jnp.minimum(i, na[0] - 1)], 0, jnp.where(live, f, n_f - 1))

    def down_idx(i, f, te, na):
        live = i < na[0]
        return (layer * n_e + te[jnp.minimum(i, na[0] - 1)], jnp.where(live, f, n_f - 1), 0)

    grid_spec = pltpu.PrefetchScalarGridSpec(
        num_scalar_prefetch=2,
        grid=(p // tm, n_f),
        in_specs=[
            pl.BlockSpec((tm, d), row_idx),
            pl.BlockSpec((tm, 1), row_idx),
            pl.BlockSpec((1, d, tf), up_idx),
            pl.BlockSpec((1, d, tf), up_idx),
            pl.BlockSpec((1, tf, d), down_idx),
        ],
        out_specs=pl.BlockSpec((tm, d), row_idx),
        scratch_shapes=[pltpu.VMEM((tm, d), F32)],
    )
    return pl.pallas_call(
        _moe_ffn_kernel,
        grid_spec=grid_spec,
        out_shape=jax.ShapeDtypeStruct((p, d), F32),
        compiler_params=_params(("arbitrary", "arbitrary")),
        name="moe_ffn",
    )(tile_expert, n_active, xs, scale, wg, wu, wd)


def _norm_router_kernel(x_ref, g_ref, wr_ref, h_ref, l_ref):
    h = _rms_normalize(x_ref[...], g_ref[...])
    h_ref[...] = h.astype(BF16)
    l_ref[...] = jnp.dot(h, wr_ref[...], preferred_element_type=F32, precision=lax.Precision.HIGHEST)


def norm_router(x, gain, router_w_padded, *, tm=512):
    t, d = x.shape
    return pl.pallas_call(
        _norm_router_kernel,
        grid=(t // tm,),
        in_specs=[
            pl.BlockSpec((tm, d), lambda i: (i, 0)),
            pl.BlockSpec((1, d), lambda i: (0, 0)),
            pl.BlockSpec((d, LANES), lambda i: (0, 0)),
        ],
        out_specs=[pl.BlockSpec((tm, d), lambda i: (i, 0)), pl.BlockSpec((tm, LANES), lambda i: (i, 0))],
        out_shape=[jax.ShapeDtypeStruct((t, d), BF16), jax.ShapeDtypeStruct((t, LANES), F32)],
        compiler_params=_params(("parallel",)),
        name="norm_router",
    )(x, gain.reshape(1, d), router_w_padded)


def _final_norm_kernel(x_ref, g_ref, o_ref):
    o_ref[...] = _rms_normalize(x_ref[...], g_ref[...])


def final_norm(x, gain, *, tm=512):
    t, d = x.shape
    return pl.pallas_call(
        _final_norm_kernel,
        grid=(t // tm,),
        in_specs=[pl.BlockSpec((tm, d), lambda i: (i, 0)), pl.BlockSpec((1, d), lambda i: (0, 0))],
        out_specs=pl.BlockSpec((tm, d), lambda i: (i, 0)),
        out_shape=jax.ShapeDtypeStruct((t, d), F32),
        compiler_params=_params(("parallel",)),
        name="final_norm",
    )(x, gain.reshape(1, d))


def moe_block(x, gain, router_w, wg, wu, wd, layer, *, tm=ROW_TILE, tf=FF_TILE):
    t, d = x.shape
    n_e = router_w.shape[1]
    wr = jnp.zeros((d, LANES), F32).at[:, :n_e].set(router_w)
    h, logits = norm_router(x, gain, wr)
    top_logit, top_idx = lax.top_k(logits[:, :n_e], TOP_K)
    top_w = jax.nn.softmax(top_logit, axis=-1)

    n_assign = t * TOP_K
    n_tiles = (n_assign + n_e * (tm - 1)) // tm + 1
    flat_e = top_idx.reshape(-1)
    order = jnp.argsort(flat_e, stable=True)
    sorted_e = flat_e[order]
    counts = jnp.zeros((n_e,), jnp.int32).at[flat_e].add(1)
    padded = ((counts + tm - 1) // tm) * tm
    pad_end = jnp.cumsum(padded)
    pad_start = pad_end - padded
    start = jnp.cumsum(counts) - counts
    dest = pad_start[sorted_e] + jnp.arange(n_assign, dtype=jnp.int32) - start[sorted_e]
    src_token = jnp.zeros((n_tiles * tm,), jnp.int32).at[dest].set((order // TOP_K).astype(jnp.int32))
    scale = jnp.zeros((n_tiles * tm,), F32).at[dest].set(top_w.reshape(-1)[order])
    pos = jnp.zeros((n_assign,), jnp.int32).at[order].set(dest).reshape(t, TOP_K)
    tile_start = jnp.arange(n_tiles, dtype=jnp.int32) * tm
    tile_expert = jnp.minimum(jnp.searchsorted(pad_end, tile_start, side="right"), n_e - 1).astype(jnp.int32)
    n_active = (pad_end[-1] // tm).astype(jnp.int32).reshape(1)

    xs = jnp.take(h, src_token, axis=0)
    ys = moe_ffn_grouped(xs, scale.reshape(-1, 1), tile_expert, n_active, wg, wu, wd, layer, tm=tm, tf=tf)
    return x + jnp.take(ys, pos[:, 0], axis=0) + jnp.take(ys, pos[:, 1], axis=0)


CONV_HALO = 8
CONV_COLS = 512


def _conv_act_kernel(prev_ref, cur_ref, next_ref, w_ref, b_ref, l2_ref, o_ref, buf, *, tiles_per_seq, head_dim):
    tm = cur_ref.shape[0]
    pos = pl.program_id(0) % tiles_per_seq
    buf[0:CONV_HALO, :] = jnp.where(pos == 0, 0.0, prev_ref[...])
    buf[CONV_HALO:CONV_HALO + tm, :] = cur_ref[...]
    buf[CONV_HALO + tm:2 * CONV_HALO + tm, :] = jnp.where(pos == tiles_per_seq - 1, 0.0, next_ref[...])
    acc = jnp.broadcast_to(b_ref[...], (tm, cur_ref.shape[1]))
    for k in range(CONV_K):
        acc = acc + buf[pl.ds(CONV_HALO - CONV_K // 2 + k, tm), :] * w_ref[k:k + 1, :]
    y = acc * jax.nn.sigmoid(acc)
    if head_dim is None:
        o_ref[...] = y
    else:
        for h in range(y.shape[1] // head_dim):
            hs = slice(h * head_dim, (h + 1) * head_dim)
            yh = y[:, hs]
            normed = yh * lax.rsqrt(jnp.sum(yh * yh, axis=-1, keepdims=True) + EPS) * l2_ref[1:2, hs]
            o_ref[:, hs] = jnp.where(l2_ref[0:1, hs] > 0.5, normed, yh)


def conv_act(u_wide, col0, width, conv_w, conv_b, batch, *, l2=None, head_dim=None, tm=256):
    t = u_wide.shape[0]
    nrow = t // tm
    nhalo = tm // CONV_HALO
    cb0 = col0 // CONV_COLS
    w = jnp.zeros((CONV_HALO, width), F32).at[:CONV_K].set(conv_w.T)
    if l2 is None:
        l2 = jnp.zeros((2, width), F32)
    l2 = jnp.zeros((CONV_HALO, width), F32).at[:2].set(l2)
    kern = functools.partial(_conv_act_kernel, tiles_per_seq=t // batch // tm, head_dim=head_dim)
    par = pl.BlockSpec((CONV_HALO, CONV_COLS), lambda i, j: (0, j))
    return pl.pallas_call(
        kern,
        grid=(nrow, width // CONV_COLS),
        in_specs=[
            pl.BlockSpec((CONV_HALO, CONV_COLS), lambda i, j: (jnp.maximum(i * nhalo - 1, 0), cb0 + j)),
            pl.BlockSpec((tm, CONV_COLS), lambda i, j: (i, cb0 + j)),
            pl.BlockSpec((CONV_HALO, CONV_COLS), lambda i, j: (jnp.minimum((i + 1) * nhalo, t // CONV_HALO - 1), cb0 + j)),
            par,
            pl.BlockSpec((1, CONV_COLS), lambda i, j: (0, j)),
            par,
        ],
        out_specs=pl.BlockSpec((tm, CONV_COLS), lambda i, j: (i, j)),
        out_shape=jax.ShapeDtypeStruct((t, width), F32),
        scratch_shapes=[pltpu.VMEM((tm + 2 * CONV_HALO, CONV_COLS), F32)],
        compiler_params=_params(("parallel", "parallel")),
        name="conv_act",
    )(u_wide, u_wide, u_wide, w, conv_b.reshape(1, width), l2)


NARROW_GDN_A = 2 * SSD_HEADS
NARROW_GDN_B = NARROW_GDN_A + 2 * GDN_HEADS


def _mm(a, b):
    return jnp.dot(a.astype(BF16), b.astype(BF16), preferred_element_type=F32)


def _mm_nt(a, b):
    return lax.dot_general(a.astype(BF16), b.astype(BF16), (((1,), (1,)), ((), ())), preferred_element_type=F32)


def _mm_tn(a, b):
    return lax.dot_general(a.astype(BF16), b.astype(BF16), (((0,), (0,)), ((), ())), preferred_element_type=F32)


def _unit_triangular_inverses(lmats, row, col):
    q = lmats[0].shape[0]
    eye = (row == col).astype(F32)

    def same_block(shift):
        return (row >> shift) == (col >> shift)

    blk = same_block(3)
    ds = [jnp.where(blk, m, 0.0) for m in lmats]
    d2s = [_mm(d, d) for d in ds]
    d4s = [_mm(d2, d2) for d2 in d2s]
    xs = [_mm(eye - d, eye + d2) for d, d2 in zip(ds, d2s)]
    xs = [_mm(x, eye + d4) for x, d4 in zip(xs, d4s)]
    shift = 4
    while (1 << shift) <= q:
        level = same_block(shift) & jnp.logical_not(same_block(shift - 1))
        xes = [_mm(x, jnp.where(level, m, 0.0)) for x, m in zip(xs, lmats)]
        xs = [x - _mm(xe, x) for x, xe in zip(xs, xes)]
        shift += 1
    return xs


def _gdn_chunk_operands(q_ref, k_ref, v_ref, n_ref, ap_ref, bias_ref, direction):
    cq = q_ref.shape[0]
    row = lax.broadcasted_iota(jnp.int32, (cq, cq), 0)
    col = lax.broadcasted_iota(jnp.int32, (cq, cq), 1)
    if direction == 0:
        incl, strict = col <= row, col < row
    else:
        incl, strict = col >= row, col > row
    nar = n_ref[...]
    g_all = ap_ref[...] * jax.nn.softplus(nar + bias_ref[...])
    beta_all = jax.nn.sigmoid(nar)
    cum_op = jnp.concatenate([incl.astype(F32), jnp.ones((cq, cq), F32)], axis=0)
    cums = jnp.dot(cum_op, g_all, preferred_element_type=F32, precision=lax.Precision.HIGHEST)
    gc_all, tot_all = cums[:cq], cums[cq:]
    gc_t = gc_all.T
    egc_all = jnp.exp(gc_all)
    erem_all = jnp.exp(tot_all - gc_all)
    etot_all = jnp.exp(tot_all)
    heads = range(GDN_HEADS)
    la = [NARROW_GDN_A + direction * GDN_HEADS + h for h in heads]
    lb = [NARROW_GDN_B + direction * GDN_HEADS + h for h in heads]
    hs = [slice(h * GDN_HEAD_DIM, (h + 1) * GDN_HEAD_DIM) for h in heads]
    q = [q_ref[:, hs[h]] for h in heads]
    k = [k_ref[:, hs[h]] for h in heads]
    v = [v_ref[:, hs[h]] for h in heads]
    beta = [beta_all[:, lb[h]:lb[h] + 1] for h in heads]
    egc = [egc_all[:, la[h]:la[h] + 1] for h in heads]
    decay = [jnp.exp(jnp.where(incl, gc_all[:, la[h]:la[h] + 1] - gc_t[la[h]:la[h] + 1, :], -1e30)) for h in heads]
    kb = [k[h] * beta[h] for h in heads]
    return dict(
        q=q, k=k, decay=decay, strict=[strict] * GDN_HEADS,
        qe=[q[h] * egc[h] for h in heads],
        rhs=[jnp.concatenate([v[h] * beta[h], kb[h] * egc[h]], axis=-1) for h in heads],
        kb=kb,
        ke=[k[h] * erem_all[:, la[h]:la[h] + 1] for h in heads],
        etot=[etot_all[0:1, la[h]:la[h] + 1] for h in heads],
    )


def _gdn_scan_kernel(qf, kf, vf, nf, qb, kb, vb, nb, ap_ref, bias_ref, of_ref, ob_ref, s_ref):
    @pl.when(pl.program_id(1) == 0)
    def _():
        s_ref[...] = jnp.zeros_like(s_ref)

    cq = qf.shape[0]
    row = lax.broadcasted_iota(jnp.int32, (cq, cq), 0)
    col = lax.broadcasted_iota(jnp.int32, (cq, cq), 1)
    fwd = _gdn_chunk_operands(qf, kf, vf, nf, ap_ref, bias_ref, 0)
    bwd = _gdn_chunk_operands(qb, kb, vb, nb, ap_ref, bias_ref, 1)
    c = {name: fwd[name] + bwd[name] for name in fwd}
    n = range(2 * GDN_HEADS)
    s = [s_ref[i] for i in n]
    kk = [_mm_nt(c['kb'][i], c['k'][i]) for i in n]
    qk = [_mm_nt(c['q'][i], c['k'][i]) for i in n]
    lmats = [jnp.where(c['strict'][i], kk[i] * c['decay'][i], 0.0) for i in n]
    t_inv = _unit_triangular_inverses(lmats, row, col)
    uw = [_mm(t_inv[i], c['rhs'][i]) for i in n]
    ws = [_mm(uw[i][:, GDN_HEAD_DIM:], s[i]) for i in n]
    qs = [_mm(c['qe'][i], s[i]) for i in n]
    v_new = [uw[i][:, :GDN_HEAD_DIM] - ws[i] for i in n]
    outs = [qs[i] + _mm(qk[i] * c['decay'][i], v_new[i]) for i in n]
    upd = [_mm_tn(c['ke'][i], v_new[i]) for i in n]
    of_ref[...] = jnp.concatenate(outs[:GDN_HEADS], axis=-1)
    ob_ref[...] = jnp.concatenate(outs[GDN_HEADS:], axis=-1)
    for i in n:
        s_ref[i] = s[i] * c['etot'][i] + upd[i]


def gdn_scan(qkv, u_narrow, a_par, bias_par, batch, *, cq=GDN_CHUNK):
    t = qkv.shape[0]
    w = GDN_WIDTH
    nc = t // batch // cq

    def fwd(col):
        return lambda b, c: (b * nc + c, col)

    def bwd(col):
        return lambda b, c: (b * nc + nc - 1 - c, col)

    wide = lambda idx: pl.BlockSpec((cq, w), idx)
    narrow = lambda idx: pl.BlockSpec((cq, LANES), idx)
    par = pl.BlockSpec((1, LANES), lambda b, c: (0, 0))
    return pl.pallas_call(
        _gdn_scan_kernel,
        grid=(batch, nc),
        in_specs=[wide(fwd(0)), wide(fwd(1)), wide(fwd(2)), narrow(fwd(0)),
                  wide(bwd(0)), wide(bwd(1)), wide(bwd(2)), narrow(bwd(0)), par, par],
        out_specs=[wide(fwd(0)), wide(bwd(0))],
        out_shape=[jax.ShapeDtypeStruct((t, w), F32), jax.ShapeDtypeStruct((t, w), F32)],
        scratch_shapes=[pltpu.VMEM((2 * GDN_HEADS, GDN_HEAD_DIM, GDN_HEAD_DIM), F32)],
        compiler_params=_params(("arbitrary", "arbitrary")),
        name="gdn_scan",
    )(qkv, qkv, qkv, u_narrow, qkv, qkv, qkv, u_narrow, a_par, bias_par)


def _gdn_gate_kernel(of_ref, ob_ref, z_ref, w_ref, o_ref):
    for h in range(GDN_HEADS):
        hs = slice(h * GDN_HEAD_DIM, (h + 1) * GDN_HEAD_DIM)
        o = _rms_normalize(of_ref[:, hs] + ob_ref[:, hs], w_ref[...])
        z = z_ref[:, hs]
        o_ref[:, hs] = o * (z * jax.nn.sigmoid(z))


def gdn_gate(o_f, o_b, u_wide, norm_w, *, tm=512):
    t, w = o_f.shape
    zblk = U_GDN_Z // GDN_WIDTH
    row = lambda i: (i, 0)
    return pl.pallas_call(
        _gdn_gate_kernel,
        grid=(t // tm,),
        in_specs=[pl.BlockSpec((tm, w), row), pl.BlockSpec((tm, w), row), pl.BlockSpec((tm, w), lambda i: (i, zblk)),
                  pl.BlockSpec((1, GDN_HEAD_DIM), lambda i: (0, 0))],
        out_specs=pl.BlockSpec((tm, w), row),
        out_shape=jax.ShapeDtypeStruct((t, w), F32),
        compiler_params=_params(("parallel",)),
        name="gdn_gate",
    )(o_f, o_b, u_wide, norm_w.reshape(1, -1))


def gdn_mixer(u_wide, u_narrow, batch, conv_w, dt_bias, a_log, norm_w):
    ones, zeros = jnp.ones((GDN_WIDTH,), F32), jnp.zeros((GDN_WIDTH,), F32)
    l2 = jnp.stack([jnp.concatenate([ones, ones, zeros]),
                    jnp.concatenate([ones * GDN_HEAD_DIM ** -0.5, ones, zeros])])
    qkv = conv_act(u_wide, U_GDN_QKV, GDN_CONV_DIM, conv_w, jnp.zeros((GDN_CONV_DIM,), F32), batch,
                   l2=l2, head_dim=GDN_HEAD_DIM)
    lanes = slice(NARROW_GDN_A, NARROW_GDN_B)
    a_par = jnp.zeros((1, LANES), F32).at[0, lanes].set(-jnp.exp(a_log).reshape(-1))
    bias_par = jnp.zeros((1, LANES), F32).at[0, lanes].set(dt_bias.reshape(-1))
    o_f, o_b = gdn_scan(qkv, u_narrow, a_par, bias_par, batch)
    return gdn_gate(o_f, o_b, u_wide, norm_w)


SSD_BC = SSD_GROUPS * SSD_STATE
SSD_REP = SSD_HEADS // SSD_GROUPS


def _ssd_chunk_operands(xbc_ref, n_ref, ap_ref, bias_ref, direction):
    cq = xbc_ref.shape[0]
    row = lax.broadcasted_iota(jnp.int32, (cq, cq), 0)
    col = lax.broadcasted_iota(jnp.int32, (cq, cq), 1)
    incl = col <= row if direction == 0 else col >= row
    dt_all = jax.nn.softplus(n_ref[...] + bias_ref[...])
    cum_op = jnp.concatenate([incl.astype(F32), jnp.ones((cq, cq), F32)], axis=0)
    cums = jnp.dot(cum_op, dt_all * ap_ref[...], preferred_element_type=F32, precision=lax.Precision.HIGHEST)
    ac_all, tot_all = cums[:cq], cums[cq:]
    ac_t = ac_all.T
    eac_all = jnp.exp(ac_all)
    erem_all = jnp.exp(tot_all - ac_all)
    etot_all = jnp.exp(tot_all)
    groups = range(SSD_GROUPS)
    bm = [xbc_ref[:, SSD_WIDTH + g * SSD_STATE:SSD_WIDTH + (g + 1) * SSD_STATE] for g in groups]
    cm = [xbc_ref[:, SSD_WIDTH + SSD_BC + g * SSD_STATE:SSD_WIDTH + SSD_BC + (g + 1) * SSD_STATE] for g in groups]
    cb = [_mm_nt(cm[g], bm[g]) for g in groups]
    heads = range(SSD_HEADS)
    ln = [direction * SSD_HEADS + h for h in heads]
    seg = [jnp.exp(jnp.where(incl, ac_all[:, ln[h]:ln[h] + 1] - ac_t[ln[h]:ln[h] + 1, :], -1e30)) for h in heads]
    return dict(
        scores=[cb[h // SSD_REP] * seg[h] for h in heads],
        xdt=[xbc_ref[:, h * SSD_HEAD_DIM:(h + 1) * SSD_HEAD_DIM] * dt_all[:, ln[h]:ln[h] + 1] for h in heads],
        b_end=[bm[h // SSD_REP] * erem_all[:, ln[h]:ln[h] + 1] for h in heads],
        c_in=[cm[h // SSD_REP] * eac_all[:, ln[h]:ln[h] + 1] for h in heads],
        etot=[etot_all[0:1, ln[h]:ln[h] + 1] for h in heads],
    )


def _ssd_scan_kernel(xf, nf, xb, nb, ap_ref, bias_ref, of_ref, ob_ref, s_ref):
    @pl.when(pl.program_id(1) == 0)
    def _():
        s_ref[...] = jnp.zeros_like(s_ref)

    fwd = _ssd_chunk_operands(xf, nf, ap_ref, bias_ref, 0)
    bwd = _ssd_chunk_operands(xb, nb, ap_ref, bias_ref, 1)
    c = {name: fwd[name] + bwd[name] for name in fwd}
    n = range(2 * SSD_HEADS)
    s = [s_ref[i] for i in n]
    y = [_mm(c['scores'][i], c['xdt'][i]) + _mm(c['c_in'][i], s[i]) for i in n]
    upd = [_mm_tn(c['b_end'][i], c['xdt'][i]) for i in n]
    of_ref[...] = jnp.concatenate(y[:SSD_HEADS], axis=-1)
    ob_ref[...] = jnp.concatenate(y[SSD_HEADS:], axis=-1)
    for i in n:
        s_ref[i] = s[i] * c['etot'][i] + upd[i]


def ssd_scan(xbc, u_narrow, a_par, bias_par, batch, *, cq=SSD_CHUNK):
    t = xbc.shape[0]
    nc = t // batch // cq

    def fwd(b, c):
        return (b * nc + c, 0)

    def bwd(b, c):
        return (b * nc + nc - 1 - c, 0)

    wide = lambda idx: pl.BlockSpec((cq, SSD_CONV_DIM), idx)
    narrow = lambda idx: pl.BlockSpec((cq, LANES), idx)
    out = lambda idx: pl.BlockSpec((cq, SSD_WIDTH), idx)
    par = pl.BlockSpec((1, LANES), lambda b, c: (0, 0))
    return pl.pallas_call(
        _ssd_scan_kernel,
        grid=(batch, nc),
        in_specs=[wide(fwd), narrow(fwd), wide(bwd), narrow(bwd), par, par],
        out_specs=[out(fwd), out(bwd)],
        out_shape=[jax.ShapeDtypeStruct((t, SSD_WIDTH), F32), jax.ShapeDtypeStruct((t, SSD_WIDTH), F32)],
        scratch_shapes=[pltpu.VMEM((2 * SSD_HEADS, SSD_STATE, SSD_HEAD_DIM), F32)],
        compiler_params=_params(("arbitrary", "arbitrary")),
        name="ssd_scan",
    )(xbc, u_narrow, xbc, u_narrow, a_par, bias_par)


def _ssd_gate_kernel(yf_ref, yb_ref, x_ref, z_ref, d_ref, w_ref, o_ref):
    z = z_ref[...]
    y = (yf_ref[...] + yb_ref[...] + x_ref[...] * d_ref[...]) * (z * jax.nn.sigmoid(z))
    o_ref[...] = _rms_normalize(y, w_ref[...])


def ssd_gate(y_f, y_b, xbc, u_wide, d_wide, norm_w, *, tm=512):
    t, w = y_f.shape
    row = lambda i: (i, 0)
    vec = pl.BlockSpec((1, w), lambda i: (0, 0))
    return pl.pallas_call(
        _ssd_gate_kernel,
        grid=(t // tm,),
        in_specs=[pl.BlockSpec((tm, w), row), pl.BlockSpec((tm, w), row), pl.BlockSpec((tm, w), row),
                  pl.BlockSpec((tm, w), lambda i: (i, U_SSD_Z // SSD_WIDTH)), vec, vec],
        out_specs=pl.BlockSpec((tm, w), row),
        out_shape=jax.ShapeDtypeStruct((t, w), F32),
        compiler_params=_params(("parallel",)),
        name="ssd_gate",
    )(y_f, y_b, xbc, u_wide, d_wide, norm_w.reshape(1, -1))


def ssd_mixer(u_wide, u_narrow, batch, conv_w, conv_b, dt_bias, a_log, d_skip, norm_w):
    xbc = conv_act(u_wide, U_SSD_XBC, SSD_CONV_DIM, conv_w, conv_b, batch)
    lanes = slice(0, 2 * SSD_HEADS)
    a_par = jnp.zeros((1, LANES), F32).at[0, lanes].set(-jnp.exp(a_log).reshape(-1))
    bias_par = jnp.zeros((1, LANES), F32).at[0, lanes].set(dt_bias.reshape(-1))
    y_f, y_b = ssd_scan(xbc, u_narrow, a_par, bias_par, batch)
    d_wide = jnp.repeat(d_skip, SSD_HEAD_DIM).reshape(1, SSD_WIDTH)
    return ssd_gate(y_f, y_b, xbc, u_wide, d_wide, norm_w)


FNET_N2 = 128
FNET_PAIR = 2 * FNET_HEAD_DIM
FNET_SLAB = 16


def _dft_cos_sin(n, scale):
    idx = jnp.arange(n, dtype=jnp.int32)
    ang = (2.0 * math.pi / n) * ((idx[:, None] * idx[None, :]) % n).astype(F32)
    return jnp.cos(ang) * scale, jnp.sin(ang) * scale


def _fnet_channel_kernel(u_ref, m_ref, y_ref):
    for h in range(FNET_HEADS):
        uh = u_ref[:, h * FNET_HEAD_DIM:(h + 1) * FNET_HEAD_DIM].astype(BF16)
        y_ref[h] = jnp.dot(uh, m_ref[...], preferred_element_type=F32).astype(BF16)


def _re_im_products(c_ref, s_ref, x):
    p = jnp.dot(c_ref[...], x, preferred_element_type=F32)
    q = jnp.dot(s_ref[...], x, preferred_element_type=F32)
    res = []
    for j in range(x.shape[1] // FNET_PAIR):
        re = slice(j * FNET_PAIR, j * FNET_PAIR + FNET_HEAD_DIM)
        im = slice(j * FNET_PAIR + FNET_HEAD_DIM, (j + 1) * FNET_PAIR)
        res.append((p[:, re] + q[:, im], p[:, im] - q[:, re]))
    return res


def _fnet_stage1_kernel(x_ref, c_ref, s_ref, tc_ref, ts_ref, o_ref):
    for j, (ar, ai) in enumerate(_re_im_products(c_ref, s_ref, x_ref[0])):
        lanes = slice(j * FNET_HEAD_DIM, (j + 1) * FNET_HEAD_DIM)
        tc, ts = tc_ref[:, lanes], ts_ref[:, lanes]
        o_ref[0, j] = jnp.concatenate([ar * tc + ai * ts, ai * tc - ar * ts], axis=-1).astype(BF16)


def _fnet_stage2_kernel(x_ref, c_ref, s_ref, o_ref):
    o_ref[0] = jnp.concatenate([re for re, _ in _re_im_products(c_ref, s_ref, x_ref[0])], axis=-1)


def _fnet_out_kernel(f_ref, w_ref, b_ref, o_ref):
    acc = b_ref[...]
    for h in range(FNET_HEADS):
        acc = acc + jnp.dot(f_ref[h].astype(BF16), w_ref[h * FNET_HEAD_DIM:(h + 1) * FNET_HEAD_DIM, :],
                            preferred_element_type=F32)
    o_ref[...] = acc


def fourier_mixer(u_wide, batch, w, b, *, tm=512, slab=FNET_SLAB):
    t = u_wide.shape[0]
    seq = t // batch
    n1, n2 = seq // FNET_N2, FNET_N2
    hb = FNET_HEADS * batch
    cc, sc = _dft_cos_sin(FNET_HEAD_DIM, FNET_HEAD_DIM ** -0.5)
    y = pl.pallas_call(
        _fnet_channel_kernel,
        grid=(t // tm,),
        in_specs=[pl.BlockSpec((tm, FNET_WIDTH), lambda i: (i, U_FNET // FNET_WIDTH)),
                  pl.BlockSpec((FNET_HEAD_DIM, FNET_PAIR), lambda i: (0, 0))],
        out_specs=pl.BlockSpec((FNET_HEADS, tm, FNET_PAIR), lambda i: (0, i, 0)),
        out_shape=jax.ShapeDtypeStruct((FNET_HEADS, t, FNET_PAIR), BF16),
        compiler_params=_params(("parallel",)),
        name="fnet_channel",
    )(u_wide, jnp.concatenate([cc, -sc], axis=-1).astype(BF16))

    c1, s1 = (m.astype(BF16) for m in _dft_cos_sin(n1, n1 ** -0.5))
    k1 = jnp.arange(n1, dtype=jnp.int32)[:, None]
    m2 = jnp.arange(n2, dtype=jnp.int32)[None, :]
    ang = (2.0 * math.pi / seq) * ((k1 * m2) % seq).astype(F32)
    tw_c = jnp.repeat(jnp.cos(ang), FNET_HEAD_DIM, axis=1)
    tw_s = jnp.repeat(jnp.sin(ang), FNET_HEAD_DIM, axis=1)
    mat1 = pl.BlockSpec((n1, n1), lambda g, j: (0, 0))
    a = pl.pallas_call(
        _fnet_stage1_kernel,
        grid=(hb, n2 // slab),
        in_specs=[pl.BlockSpec((1, n1, slab * FNET_PAIR), lambda g, j: (g, 0, j)), mat1, mat1,
                  pl.BlockSpec((n1, slab * FNET_HEAD_DIM), lambda g, j: (0, j)),
                  pl.BlockSpec((n1, slab * FNET_HEAD_DIM), lambda g, j: (0, j))],
        out_specs=pl.BlockSpec((1, slab, n1, FNET_PAIR), lambda g, j: (g, j, 0, 0)),
        out_shape=jax.ShapeDtypeStruct((hb, n2, n1, FNET_PAIR), BF16),
        compiler_params=_params(("parallel", "parallel")),
        name="fnet_stage1",
    )(y.reshape(hb, n1, n2 * FNET_PAIR), c1, s1, tw_c, tw_s)

    c2, s2 = (m.astype(BF16) for m in _dft_cos_sin(n2, n2 ** -0.5))
    kslab = min(slab, n1)
    mat2 = pl.BlockSpec((n2, n2), lambda g, j: (0, 0))
    f = pl.pallas_call(
        _fnet_stage2_kernel,
        grid=(hb, n1 // kslab),
        in_specs=[pl.BlockSpec((1, n2, kslab * FNET_PAIR), lambda g, j: (g, 0, j)), mat2, mat2],
        out_specs=pl.BlockSpec((1, n2, kslab * FNET_HEAD_DIM), lambda g, j: (g, 0, j)),
        out_shape=jax.ShapeDtypeStruct((hb, n2, n1 * FNET_HEAD_DIM), F32),
        compiler_params=_params(("parallel", "parallel")),
        name="fnet_stage2",
    )(a.reshape(hb, n2, n1 * FNET_PAIR), c2, s2)

    return pl.pallas_call(
        _fnet_out_kernel,
        grid=(t // tm,),
        in_specs=[pl.BlockSpec((FNET_HEADS, tm, FNET_HEAD_DIM), lambda i: (0, i, 0)),
                  pl.BlockSpec((FNET_WIDTH, FNET_WIDTH), lambda i: (0, 0)),
                  pl.BlockSpec((1, FNET_WIDTH), lambda i: (0, 0))],
        out_specs=pl.BlockSpec((tm, FNET_WIDTH), lambda i: (i, 0)),
        out_shape=jax.ShapeDtypeStruct((t, FNET_WIDTH), F32),
        compiler_params=_params(("parallel",)),
        name="fnet_out",
    )(f.reshape(FNET_HEADS, t, FNET_HEAD_DIM), w.astype(BF16), b.reshape(1, -1))


S5_TILE = 256
S5_NSTATE = S5_GROUPS * S5_STATE
S5_COLS = 512
S5_BLOCKS = S5_WIDTH // LANES


def _s5_scan_kernel(u_ref, pi_ref, pit_ref, b_ref, c_ref, lre_ref, lim_ref, o_ref,
                    sre, sim, pre, pim, car_re, car_im, e_re, e_im, cin_re, cin_im):
    tl = u_ref.shape[0]
    steps = tl // 8
    first_tile = pl.program_id(2) == 0

    def col(j):
        return slice(j * S5_COLS, (j + 1) * S5_COLS)

    def rows(t):
        return pl.ds(pl.multiple_of(t * 8, 8), 8)

    def bcast(ref, j):
        return jnp.broadcast_to(ref[0, :, col(j)], (8, S5_COLS))

    @pl.when(first_tile)
    def _():
        car_re[...] = jnp.zeros_like(car_re)
        car_im[...] = jnp.zeros_like(car_im)
        for j in range(S5_BLOCKS):
            lr, li = bcast(lre_ref, j), bcast(lim_ref, j)

            def power_step(t, p, j=j, lr=lr, li=li):
                pr, pi_ = p
                pre[rows(t), col(j)] = pr
                pim[rows(t), col(j)] = pi_
                return pr * lr - pi_ * li, pr * li + pi_ * lr

            lax.fori_loop(0, steps, power_step, (lr, li))

    up = jnp.dot(pi_ref[0], u_ref[...].astype(BF16), preferred_element_type=F32).astype(BF16)
    for j in range(S5_BLOCKS):
        uj = up[:, j * LANES:(j + 1) * LANES]
        sre[:, col(j)] = jnp.dot(uj, b_ref[0, j, 0], preferred_element_type=F32)
        sim[:, col(j)] = jnp.dot(uj, b_ref[0, j, 1], preferred_element_type=F32)

    for j in range(S5_BLOCKS):
        lr, li = bcast(lre_ref, j), bcast(lim_ref, j)

        def scan_step(t, s, j=j, lr=lr, li=li):
            sr, si = s
            nr = lr * sr - li * si + sre[rows(t), col(j)]
            ni = lr * si + li * sr + sim[rows(t), col(j)]
            sre[rows(t), col(j)] = nr
            sim[rows(t), col(j)] = ni
            return nr, ni

        zero = jnp.zeros((8, S5_COLS), F32)
        er, ei = lax.fori_loop(0, steps, scan_step, (zero, zero), unroll=4)
        e_re[:, col(j)] = er
        e_im[:, col(j)] = ei

    last = pl.ds((steps - 1) * 8, 1)
    lsr, lsi = pre[last, :], pim[last, :]
    cr, ci = car_re[...], car_im[...]
    for r in range(8):
        cin_re[r:r + 1, :] = cr
        cin_im[r:r + 1, :] = ci
        cr, ci = (e_re[r:r + 1, :] + lsr * cr - lsi * ci, e_im[r:r + 1, :] + lsr * ci + lsi * cr)
    car_re[...] = cr
    car_im[...] = ci

    for j in range(S5_BLOCKS):
        cr, ci = cin_re[:, col(j)], cin_im[:, col(j)]

        def fix_step(t, carry, j=j, cr=cr, ci=ci):
            pr, pi_ = pre[rows(t), col(j)], pim[rows(t), col(j)]
            sre[rows(t), col(j)] = sre[rows(t), col(j)] + pr * cr - pi_ * ci
            sim[rows(t), col(j)] = sim[rows(t), col(j)] + pr * ci + pi_ * cr
            return carry

        lax.fori_loop(0, steps, fix_step, 0, unroll=4)

    ys = []
    for j in range(S5_BLOCKS):
        ys.append(jnp.dot(sre[:, col(j)].astype(BF16), c_ref[0, j, 0], preferred_element_type=F32)
                  + jnp.dot(sim[:, col(j)].astype(BF16), c_ref[0, j, 1], preferred_element_type=F32))
    y = jnp.concatenate(ys, axis=-1)
    y_hi = y.astype(BF16)
    y_lo = (y - y_hi.astype(F32)).astype(BF16)
    pit = pit_ref[0]
    o_ref[0] = jnp.dot(pit, y_hi, preferred_element_type=F32) + jnp.dot(pit, y_lo, preferred_element_type=F32)


def _s5_operators(lam_re, lam_im, log_step, b_re, b_im, c_re, c_im, tl):
    lam = lax.complex(lam_re, lam_im)
    delta = jnp.exp(log_step)[..., None]
    lam_bar = jnp.exp(lam * delta)
    b_scale = (lam_bar - 1.0) / lam
    eye = jnp.eye(8, dtype=F32)
    gpb = S5_GROUPS // S5_BLOCKS
    bd = lax.complex(b_re, b_im)[None] * b_scale[..., None]
    bd = bd.reshape(2, S5_BLOCKS, gpb, S5_STATE, S5_GROUP_CH)
    bmat = jnp.stack([jnp.einsum('djgpc,gh->djgchp', part, eye).reshape(2, S5_BLOCKS, LANES, S5_COLS)
                      for part in (bd.real, bd.imag)], axis=2).astype(BF16)
    cd = lax.complex(c_re, c_im).reshape(2, S5_BLOCKS, gpb, S5_GROUP_CH, S5_STATE)
    cmat = jnp.stack([jnp.einsum('djgcp,gh->djhpgc', part, eye).reshape(2, S5_BLOCKS, S5_COLS, LANES)
                      for part in (cd.real, -cd.imag)], axis=2).astype(BF16)
    lre = lam_bar.real.reshape(2, 1, S5_NSTATE)
    lim = lam_bar.imag.reshape(2, 1, S5_NSTATE)
    steps = tl // 8
    rt = jnp.arange(tl)
    src = (rt % 8) * steps + rt // 8
    fwd = (src[:, None] == rt[None, :])
    bwd = ((tl - 1 - src)[:, None] == rt[None, :])
    pi = jnp.stack([fwd, bwd]).astype(BF16)
    return pi, jnp.swapaxes(pi, 1, 2), bmat, cmat, lre, lim


def s5_scan(u_wide, ops, batch, *, tl=S5_TILE):
    pi, pit, bmat, cmat, lre, lim = ops
    t = u_wide.shape[0]
    nt = t // batch // tl
    ublk = U_S5 // S5_WIDTH

    def tile_idx(b, d, i):
        return b * nt + jnp.where(d == 0, i, nt - 1 - i)

    state = pltpu.VMEM((tl, S5_NSTATE), F32)
    row = pltpu.VMEM((1, S5_NSTATE), F32)
    seg = pltpu.VMEM((8, S5_NSTATE), F32)
    return pl.pallas_call(
        _s5_scan_kernel,
        grid=(batch, 2, nt),
        in_specs=[
            pl.BlockSpec((tl, S5_WIDTH), lambda b, d, i: (tile_idx(b, d, i), ublk)),
            pl.BlockSpec((1, tl, tl), lambda b, d, i: (d, 0, 0)),
            pl.BlockSpec((1, tl, tl), lambda b, d, i: (d, 0, 0)),
            pl.BlockSpec((1, S5_BLOCKS, 2, LANES, S5_COLS), lambda b, d, i: (d, 0, 0, 0, 0)),
            pl.BlockSpec((1, S5_BLOCKS, 2, S5_COLS, LANES), lambda b, d, i: (d, 0, 0, 0, 0)),
            pl.BlockSpec((1, 1, S5_NSTATE), lambda b, d, i: (d, 0, 0)),
            pl.BlockSpec((1, 1, S5_NSTATE), lambda b, d, i: (d, 0, 0)),
        ],
        out_specs=pl.BlockSpec((1, tl, S5_WIDTH), lambda b, d, i: (d, tile_idx(b, d, i), 0)),
        out_shape=jax.ShapeDtypeStruct((2, t, S5_WIDTH), F32),
        scratch_shapes=[state, state, state, state, row, row, seg, seg, seg, seg],
        compiler_params=_params(("arbitrary", "arbitrary", "arbitrary")),
        name="s5_scan",
    )(u_wide, pi, pit, bmat, cmat, lre, lim)


def _s5_gate_kernel(y_ref, u_ref, d_ref, w_ref, b_ref, o_ref):
    y = y_ref[0] + y_ref[1] + u_ref[...] * d_ref[...]
    h = jax.nn.gelu(y)
    gate = jnp.dot(h.astype(BF16), w_ref[...], preferred_element_type=F32) + b_ref[...]
    o_ref[...] = h * jax.nn.sigmoid(gate)


def s5_gate(y2, u_wide, d_skip, glu_w, glu_b, *, tm=512):
    t = u_wide.shape[0]
    ublk = U_S5 // S5_WIDTH
    return pl.pallas_call(
        _s5_gate_kernel,
        grid=(t // tm,),
        in_specs=[
            pl.BlockSpec((2, tm, S5_WIDTH), lambda i: (0, i, 0)),
            pl.BlockSpec((tm, S5_WIDTH), lambda i: (i, ublk)),
            pl.BlockSpec((1, S5_WIDTH), lambda i: (0, 0)),
            pl.BlockSpec((S5_WIDTH, S5_WIDTH), lambda i: (0, 0)),
            pl.BlockSpec((1, S5_WIDTH), lambda i: (0, 0)),
        ],
        out_specs=pl.BlockSpec((tm, S5_WIDTH), lambda i: (i, 0)),
        out_shape=jax.ShapeDtypeStruct((t, S5_WIDTH), F32),
        compiler_params=_params(("parallel",)),
        name="s5_gate",
    )(y2, u_wide, d_skip.reshape(1, -1), glu_w.astype(BF16), glu_b.reshape(1, -1))


def s5_mixer(u_wide, batch, ops, d_skip, glu_w, glu_b):
    return s5_gate(s5_scan(u_wide, ops, batch), u_wide, d_skip, glu_w, glu_b)


def mixers(u_wide, u_narrow, batch, p, i):
    y_ssd = ssd_mixer(u_wide, u_narrow, batch, p['ssd_conv_w'][i], p['ssd_conv_b'][i], p['ssd_dt_bias'][i],
                      p['ssd_a_log'][i], p['ssd_d'][i], p['ssd_norm'][i])
    y_gdn = gdn_mixer(u_wide, u_narrow, batch, p['gdn_conv_w'][i], p['gdn_dt_bias'][i], p['gdn_a_log'][i],
                      p['gdn_norm'][i])
    y_fnet = fourier_mixer(u_wide, batch, p['fnet_w'][i], p['fnet_b'][i])
    s5_ops = _s5_operators(p['s5_lambda_re'][i], p['s5_lambda_im'][i], p['s5_log_step'][i], p['s5_b_re'][i],
                           p['s5_b_im'][i], p['s5_c_re'][i], p['s5_c_im'][i], S5_TILE)
    y_s5 = s5_mixer(u_wide, batch, s5_ops, p['s5_d'][i], p['s5_glu_w'][i], p['s5_glu_b'][i])
    return jnp.concatenate([y_ssd, y_gdn, y_fnet, y_s5], axis=-1)


def _split_w_in(w_in):
    o1 = SSD_IN
    o2 = o1 + GDN_IN
    ssd_z = w_in[..., :SSD_WIDTH]
    ssd_xbc = w_in[..., SSD_WIDTH:SSD_WIDTH + SSD_CONV_DIM]
    ssd_dt = w_in[..., SSD_WIDTH + SSD_CONV_DIM:o1]
    gdn_qkv = w_in[..., o1:o1 + GDN_CONV_DIM]
    gdn_z = w_in[..., o1 + GDN_CONV_DIM:o1 + 4 * GDN_WIDTH]
    gdn_ab = w_in[..., o1 + 4 * GDN_WIDTH:o2]
    rest = w_in[..., o2:]
    wide = jnp.concatenate([ssd_z, ssd_xbc, gdn_qkv, gdn_z, rest], axis=-1).astype(BF16)
    pad = jnp.zeros(w_in.shape[:-1] + (LANES - NARROW,), w_in.dtype)
    narrow = jnp.concatenate([ssd_dt, gdn_ab, pad], axis=-1).astype(BF16)
    return wide, narrow


def kernel(x_prompt, x_sample, norm_mix, w_in, ssd_conv_w, ssd_conv_b, ssd_dt_bias, ssd_a_log, ssd_d, ssd_norm,
           gdn_conv_w, gdn_dt_bias, gdn_a_log, gdn_norm, fnet_w, fnet_b, s5_lambda_re, s5_lambda_im, s5_log_step,
           s5_b_re, s5_b_im, s5_c_re, s5_c_im, s5_d, s5_glu_w, s5_glu_b, w_out, norm_ffn, ffn_w_gate, ffn_w_up,
           ffn_w_down, router_w, moe_w_gate, moe_w_up, moe_w_down, norm_final):
    p = dict(ssd_conv_w=ssd_conv_w, ssd_conv_b=ssd_conv_b, ssd_dt_bias=ssd_dt_bias, ssd_a_log=ssd_a_log, ssd_d=ssd_d,
             ssd_norm=ssd_norm, gdn_conv_w=gdn_conv_w, gdn_dt_bias=gdn_dt_bias, gdn_a_log=gdn_a_log,
             gdn_norm=gdn_norm, fnet_w=fnet_w, fnet_b=fnet_b, s5_lambda_re=s5_lambda_re, s5_lambda_im=s5_lambda_im,
             s5_log_step=s5_log_step, s5_b_re=s5_b_re, s5_b_im=s5_b_im, s5_c_re=s5_c_re, s5_c_im=s5_c_im, s5_d=s5_d,
             s5_glu_w=s5_glu_w, s5_glu_b=s5_glu_b)
    bp, lp, d = x_prompt.shape
    bs, ls, _ = x_sample.shape
    tp = bp * lp
    x = jnp.concatenate([x_prompt.reshape(tp, d), x_sample.reshape(bs * ls, d)], axis=0)
    w_wide, w_narrow = _split_w_in(w_in)
    w_out_b = w_out.astype(BF16)
    ffn_w_gate, ffn_w_up, ffn_w_down = (w.astype(BF16) for w in (ffn_w_gate, ffn_w_up, ffn_w_down))
    depth = w_in.shape[0]
    for i in range(depth):
        u_wide, u_narrow = norm_inproj(x, norm_mix[i], w_wide[i], w_narrow[i])
        y = jnp.concatenate([mixers(u_wide[:tp], u_narrow[:tp], bp, p, i),
                             mixers(u_wide[tp:], u_narrow[tp:], bs, p, i)], axis=0)
        x = outproj_residual(y, w_out_b[i], x)
        j = i // 2
        if i % 2 == 0:
            x = dense_ffn(x, norm_ffn[i], ffn_w_gate, ffn_w_up, ffn_w_down, j)
        else:
            x = moe_block(x, norm_ffn[i], router_w[j], moe_w_gate, moe_w_up, moe_w_down, j)
    y = final_norm(x, norm_final)
    return y[:tp].reshape(bp, lp, d), y[tp:].reshape(bs, ls, d)
```

```python
import functools
import math

import jax
import jax.numpy as jnp
from jax import lax
from jax.experimental import pallas as pl
from jax.experimental.pallas import tpu as pltpu

F32 = jnp.float32
BF16 = jnp.bfloat16

D_MODEL = 2048
DEPTH = 4
GROUP_WIDTH = 512

SSD_WIDTH = GROUP_WIDTH
SSD_HEAD_DIM = 64
SSD_HEADS = SSD_WIDTH // SSD_HEAD_DIM
SSD_GROUPS = 2
SSD_STATE = 128
SSD_CHUNK = 128
SSD_CONV_DIM = SSD_WIDTH + 2 * SSD_GROUPS * SSD_STATE

GDN_WIDTH = GROUP_WIDTH
GDN_HEAD_DIM = 128
GDN_HEADS = GDN_WIDTH // GDN_HEAD_DIM
GDN_CHUNK = 64
GDN_CONV_DIM = 3 * GDN_WIDTH

CONV_K = 5

FNET_WIDTH = GROUP_WIDTH
FNET_HEADS = 4
FNET_HEAD_DIM = FNET_WIDTH // FNET_HEADS

S5_WIDTH = GROUP_WIDTH
S5_GROUP_CH = 16
S5_GROUPS = S5_WIDTH // S5_GROUP_CH
S5_STATE = 64

SSD_IN = 2 * SSD_WIDTH + 2 * SSD_GROUPS * SSD_STATE + 2 * SSD_HEADS
GDN_IN = 4 * GDN_WIDTH + 4 * GDN_HEADS
IN_COLS = SSD_IN + GDN_IN + FNET_WIDTH + S5_WIDTH

N_EXPERTS = 8
TOP_K = 2
EPS = 1e-6

LANES = 128
VMEM_LIMIT_BYTES = 56 * 1024 * 1024

U_SSD_Z = 0
U_SSD_XBC = U_SSD_Z + SSD_WIDTH
U_GDN_QKV = U_SSD_XBC + SSD_CONV_DIM
U_GDN_Z = U_GDN_QKV + GDN_CONV_DIM
U_FNET = U_GDN_Z + GDN_WIDTH
U_S5 = U_FNET + FNET_WIDTH
U_WIDE = U_S5 + S5_WIDTH
NARROW = 2 * SSD_HEADS + 4 * GDN_HEADS

ROW_TILE = 1024
FF_TILE = 256


def _params(semantics):
    return pltpu.CompilerParams(dimension_semantics=semantics, vmem_limit_bytes=VMEM_LIMIT_BYTES)


def _drop_refs(kernel, start, count, *refs):
    return kernel(*refs[:start], *refs[start + count:])


def _group_call(kernel, name, grid, in_specs, args, out_specs, out_shape, scratch_shapes, carried):
    aliases = {}
    if carried is not None:
        n_in = len(args)
        kernel = functools.partial(_drop_refs, kernel, n_in, len(carried))
        in_specs = list(in_specs) + [pl.BlockSpec(memory_space=pl.ANY)] * len(carried)
        args = tuple(args) + tuple(carried)
        aliases = {n_in + k: k for k in range(len(carried))}
    return pl.pallas_call(
        kernel, grid=grid, in_specs=in_specs, out_specs=out_specs, out_shape=out_shape,
        scratch_shapes=scratch_shapes, input_output_aliases=aliases,
        compiler_params=_params(("arbitrary",) * len(grid)), name=name,
    )(*args)


def _rms_normalize(x, gain):
    ms = jnp.mean(x * x, axis=-1, keepdims=True)
    return x * lax.rsqrt(ms + EPS) * gain


def _norm_inproj_kernel(x_ref, g_ref, w_ref, wn_ref, o_ref, on_ref, h_ref):
    @pl.when(pl.program_id(1) == 0)
    def _():
        h = _rms_normalize(x_ref[...], g_ref[...]).astype(BF16)
        h_ref[...] = h
        on_ref[...] = jnp.dot(h, wn_ref[...], preferred_element_type=F32)

    o_ref[...] = jnp.dot(h_ref[...], w_ref[...], preferred_element_type=F32)


def norm_inproj(x, gain, w_wide, w_narrow, *, tm=ROW_TILE, tn=512):
    t, d = x.shape
    n = w_wide.shape[1]
    return pl.pallas_call(
        _norm_inproj_kernel,
        grid=(t // tm, n // tn),
        in_specs=[
            pl.BlockSpec((tm, d), lambda i, j: (i, 0)),
            pl.BlockSpec((1, d), lambda i, j: (0, 0)),
            pl.BlockSpec((d, tn), lambda i, j: (0, j)),
            pl.BlockSpec((d, LANES), lambda i, j: (0, 0)),
        ],
        out_specs=[
            pl.BlockSpec((tm, tn), lambda i, j: (i, j)),
            pl.BlockSpec((tm, LANES), lambda i, j: (i, 0)),
        ],
        out_shape=[jax.ShapeDtypeStruct((t, n), F32), jax.ShapeDtypeStruct((t, LANES), F32)],
        scratch_shapes=[pltpu.VMEM((tm, d), BF16)],
        compiler_params=_params(("parallel", "arbitrary")),
        name="norm_inproj",
    )(x, gain.reshape(1, d), w_wide, w_narrow)


def _outproj_kernel(*refs):
    *y_refs, w_ref, x_ref, o_ref = refs
    acc = x_ref[...]
    for m, y_ref in enumerate(y_refs):
        k = y_ref.shape[1]
        acc = acc + jnp.dot(y_ref[...].astype(BF16), w_ref[m * k:(m + 1) * k, :], preferred_element_type=F32)
    o_ref[...] = acc


def outproj_residual(ys, w, x, *, tm=512):
    t, d = x.shape
    row = lambda i: (i, 0)
    return pl.pallas_call(
        _outproj_kernel,
        grid=(t // tm,),
        in_specs=[pl.BlockSpec((tm, y.shape[1]), row) for y in ys] + [
            pl.BlockSpec(w.shape, lambda i: (0, 0)),
            pl.BlockSpec((tm, d), row),
        ],
        out_specs=pl.BlockSpec((tm, d), row),
        out_shape=jax.ShapeDtypeStruct((t, d), F32),
        compiler_params=_params(("parallel",)),
        name="outproj_residual",
    )(*ys, w, x)


def _swiglu_step(h, wg_ref, wu_ref, wd_ref, acc_ref):
    g = jnp.dot(h, wg_ref[0].astype(BF16), preferred_element_type=F32)
    u = jnp.dot(h, wu_ref[0].astype(BF16), preferred_element_type=F32)
    a = (g * jax.nn.sigmoid(g) * u).astype(BF16)
    acc_ref[...] += jnp.dot(a, wd_ref[0].astype(BF16), preferred_element_type=F32)


def _dense_ffn_kernel(x_ref, g_ref, wg_ref, wu_ref, wd_ref, o_ref, h_ref, acc_ref):
    f = pl.program_id(1)

    @pl.when(f == 0)
    def _():
        h_ref[...] = _rms_normalize(x_ref[...], g_ref[...]).astype(BF16)
        acc_ref[...] = jnp.zeros_like(acc_ref)

    _swiglu_step(h_ref[...], wg_ref, wu_ref, wd_ref, acc_ref)

    @pl.when(f == pl.num_programs(1) - 1)
    def _():
        o_ref[...] = x_ref[...] + acc_ref[...]


def dense_ffn(x, gain, wg, wu, wd, layer, *, tm=512, tf=512):
    t, d = x.shape
    ff = wg.shape[2]
    return pl.pallas_call(
        _dense_ffn_kernel,
        grid=(t // tm, ff // tf),
        in_specs=[
            pl.BlockSpec((tm, d), lambda i, f: (i, 0)),
            pl.BlockSpec((1, d), lambda i, f: (0, 0)),
            pl.BlockSpec((1, d, tf), lambda i, f: (layer, 0, f)),
            pl.BlockSpec((1, d, tf), lambda i, f: (layer, 0, f)),
            pl.BlockSpec((1, tf, d), lambda i, f: (layer, f, 0)),
        ],
        out_specs=pl.BlockSpec((tm, d), lambda i, f: (i, 0)),
        out_shape=jax.ShapeDtypeStruct((t, d), F32),
        scratch_shapes=[pltpu.VMEM((tm, d), BF16), pltpu.VMEM((tm, d), F32)],
        compiler_params=_params(("parallel", "arbitrary")),
        name="dense_ffn",
    )(x, gain.reshape(1, d), wg, wu, wd)


def _moe_ffn_kernel(te_ref, na_ref, x_ref, s_ref, wg_ref, wu_ref, wd_ref, o_ref, acc_ref):
    i = pl.program_id(0)
    f = pl.program_id(1)

    @pl.when(i < na_ref[0])
    def _():
        @pl.when(f == 0)
        def _():
            acc_ref[...] = jnp.zeros_like(acc_ref)

        _swiglu_step(x_ref[...], wg_ref, wu_ref, wd_ref, acc_ref)

        @pl.when(f == pl.num_programs(1) - 1)
        def _():
            o_ref[...] = acc_ref[...] * s_ref[...]


def moe_ffn_grouped(xs, scale, tile_expert, n_active, wg, wu, wd, layer, *, tm=ROW_TILE, tf=FF_TILE):
    p, d = xs.shape
    n_e, ff = wg.shape[1], wg.shape[3]
    n_f = ff // tf
    wg = wg.reshape(-1, d, ff)
    wu = wu.reshape(-1, d, ff)
    wd = wd.reshape(-1, ff, d)

    def row_idx(i, f, te, na):
        return (jnp.minimum(i, na[0] - 1), 0)

    def up_idx(i, f, te, na):
        live = i < na[0]
        return (layer * n_e + te[jnp.minimum(i, na[0] - 1)], 0, jnp.where(live, f, n_f - 1))

    def down_idx(i, f, te, na):
        live = i < na[0]
        return (layer * n_e + te[jnp.minimum(i, na[0] - 1)], jnp.where(live, f, n_f - 1), 0)

    grid_spec = pltpu.PrefetchScalarGridSpec(
        num_scalar_prefetch=2,
        grid=(p // tm, n_f),
        in_specs=[
            pl.BlockSpec((tm, d), row_idx),
            pl.BlockSpec((tm, 1), row_idx),
            pl.BlockSpec((1, d, tf), up_idx),
            pl.BlockSpec((1, d, tf), up_idx),
            pl.BlockSpec((1, tf, d), down_idx),
        ],
        out_specs=pl.BlockSpec((tm, d), row_idx),
        scratch_shapes=[pltpu.VMEM((tm, d), F32)],
    )
    return pl.pallas_call(
        _moe_ffn_kernel,
        grid_spec=grid_spec,
        out_shape=jax.ShapeDtypeStruct((p, d), F32),
        compiler_params=_params(("arbitrary", "arbitrary")),
        name="moe_ffn",
    )(tile_expert, n_active, xs, scale, wg, wu, wd)


def _norm_router_kernel(x_ref, g_ref, wr_ref, h_ref, l_ref):
    h = _rms_normalize(x_ref[...], g_ref[...])
    h_ref[...] = h.astype(BF16)
    l_ref[...] = jnp.dot(h, wr_ref[...], preferred_element_type=F32, precision=lax.Precision.HIGHEST)


def norm_router(x, gain, router_w_padded, *, tm=512):
    t, d = x.shape
    return pl.pallas_call(
        _norm_router_kernel,
        grid=(t // tm,),
        in_specs=[
            pl.BlockSpec((tm, d), lambda i: (i, 0)),
            pl.BlockSpec((1, d), lambda i: (0, 0)),
            pl.BlockSpec((d, LANES), lambda i: (0, 0)),
        ],
        out_specs=[pl.BlockSpec((tm, d), lambda i: (i, 0)), pl.BlockSpec((tm, LANES), lambda i: (i, 0))],
        out_shape=[jax.ShapeDtypeStruct((t, d), BF16), jax.ShapeDtypeStruct((t, LANES), F32)],
        compiler_params=_params(("parallel",)),
        name="norm_router",
    )(x, gain.reshape(1, d), router_w_padded)


def _final_norm_kernel(x_ref, g_ref, o_ref):
    o_ref[...] = _rms_normalize(x_ref[...], g_ref[...])


def final_norm(x, gain, *, tm=512):
    t, d = x.shape
    return pl.pallas_call(
        _final_norm_kernel,
        grid=(t // tm,),
        in_specs=[pl.BlockSpec((tm, d), lambda i: (i, 0)), pl.BlockSpec((1, d), lambda i: (0, 0))],
        out_specs=pl.BlockSpec((tm, d), lambda i: (i, 0)),
        out_shape=jax.ShapeDtypeStruct((t, d), F32),
        compiler_params=_params(("parallel",)),
        name="final_norm",
    )(x, gain.reshape(1, d))


def moe_block(x, gain, router_w, wg, wu, wd, layer, *, tm=ROW_TILE, tf=FF_TILE):
    t, d = x.shape
    n_e = router_w.shape[1]
    wr = jnp.zeros((d, LANES), F32).at[:, :n_e].set(router_w)
    h, logits = norm_router(x, gain, wr)
    top_logit, top_idx = lax.top_k(logits[:, :n_e], TOP_K)
    top_w = jax.nn.softmax(top_logit, axis=-1)

    n_assign = t * TOP_K
    n_tiles = (n_assign + n_e * (tm - 1)) // tm + 1
    flat_e = top_idx.reshape(-1)
    order = jnp.argsort(flat_e, stable=True)
    sorted_e = flat_e[order]
    counts = jnp.zeros((n_e,), jnp.int32).at[flat_e].add(1)
    padded = ((counts + tm - 1) // tm) * tm
    pad_end = jnp.cumsum(padded)
    pad_start = pad_end - padded
    start = jnp.cumsum(counts) - counts
    dest = pad_start[sorted_e] + jnp.arange(n_assign, dtype=jnp.int32) - start[sorted_e]
    src_token = jnp.zeros((n_tiles * tm,), jnp.int32).at[dest].set((order // TOP_K).astype(jnp.int32))
    scale = jnp.zeros((n_tiles * tm,), F32).at[dest].set(top_w.reshape(-1)[order])
    pos = jnp.zeros((n_assign,), jnp.int32).at[order].set(dest).reshape(t, TOP_K)
    tile_start = jnp.arange(n_tiles, dtype=jnp.int32) * tm
    tile_expert = jnp.minimum(jnp.searchsorted(pad_end, tile_start, side="right"), n_e - 1).astype(jnp.int32)
    n_active = (pad_end[-1] // tm).astype(jnp.int32).reshape(1)

    xs = jnp.take(h, src_token, axis=0)
    ys = moe_ffn_grouped(xs, scale.reshape(-1, 1), tile_expert, n_active, wg, wu, wd, layer, tm=tm, tf=tf)
    return x + jnp.take(ys, pos[:, 0], axis=0) + jnp.take(ys, pos[:, 1], axis=0)


CONV_HALO = 8
CONV_COLS = 512


def _sequence_edges(tile, tm, groups):
    first = last = tile < 0
    for row0, batch, seq in groups:
        tiles_per_seq = seq // tm
        rel = tile - row0 // tm
        inside = jnp.logical_and(rel >= 0, rel < batch * tiles_per_seq)
        pos = jnp.maximum(rel, 0) % tiles_per_seq
        first = jnp.logical_or(first, jnp.logical_and(inside, pos == 0))
        last = jnp.logical_or(last, jnp.logical_and(inside, pos == tiles_per_seq - 1))
    return first, last


def _conv_act_kernel(prev_ref, cur_ref, next_ref, w_ref, b_ref, l2_ref, o_ref, buf, *, groups, head_dim):
    tm = cur_ref.shape[0]
    first, last = _sequence_edges(pl.program_id(0), tm, groups)
    buf[0:CONV_HALO, :] = jnp.where(first, 0.0, prev_ref[...])
    buf[CONV_HALO:CONV_HALO + tm, :] = cur_ref[...]
    buf[CONV_HALO + tm:2 * CONV_HALO + tm, :] = jnp.where(last, 0.0, next_ref[...])
    acc = jnp.broadcast_to(b_ref[...], (tm, cur_ref.shape[1]))
    for k in range(CONV_K):
        acc = acc + buf[pl.ds(CONV_HALO - CONV_K // 2 + k, tm), :] * w_ref[k:k + 1, :]
    y = acc * jax.nn.sigmoid(acc)
    if head_dim is None:
        o_ref[...] = y
    else:
        for h in range(y.shape[1] // head_dim):
            hs = slice(h * head_dim, (h + 1) * head_dim)
            yh = y[:, hs]
            normed = yh * lax.rsqrt(jnp.sum(yh * yh, axis=-1, keepdims=True) + EPS) * l2_ref[1:2, hs]
            o_ref[:, hs] = jnp.where(l2_ref[0:1, hs] > 0.5, normed, yh)


def conv_act(u_wide, col0, width, conv_w, conv_b, groups, *, l2=None, head_dim=None, tm=1024):
    tm = min([tm] + [seq for _, _, seq in groups])
    t = u_wide.shape[0]
    nrow = t // tm
    nhalo = tm // CONV_HALO
    cb0 = col0 // CONV_COLS
    w = jnp.zeros((CONV_HALO, width), F32).at[:CONV_K].set(conv_w.T)
    if l2 is None:
        l2 = jnp.zeros((2, width), F32)
    l2 = jnp.zeros((CONV_HALO, width), F32).at[:2].set(l2)
    kern = functools.partial(_conv_act_kernel, groups=groups, head_dim=head_dim)
    par = pl.BlockSpec((CONV_HALO, CONV_COLS), lambda i, j: (0, j))
    return pl.pallas_call(
        kern,
        grid=(nrow, width // CONV_COLS),
        in_specs=[
            pl.BlockSpec((CONV_HALO, CONV_COLS), lambda i, j: (jnp.maximum(i * nhalo - 1, 0), cb0 + j)),
            pl.BlockSpec((tm, CONV_COLS), lambda i, j: (i, cb0 + j)),
            pl.BlockSpec((CONV_HALO, CONV_COLS), lambda i, j: (jnp.minimum((i + 1) * nhalo, t // CONV_HALO - 1), cb0 + j)),
            par,
            pl.BlockSpec((1, CONV_COLS), lambda i, j: (0, j)),
            par,
        ],
        out_specs=pl.BlockSpec((tm, CONV_COLS), lambda i, j: (i, j)),
        out_shape=jax.ShapeDtypeStruct((t, width), F32),
        scratch_shapes=[pltpu.VMEM((tm + 2 * CONV_HALO, CONV_COLS), F32)],
        compiler_params=_params(("parallel", "parallel")),
        name="conv_act",
    )(u_wide, u_wide, u_wide, w, conv_b.reshape(1, width), l2)


NARROW_GDN_A = 2 * SSD_HEADS
NARROW_GDN_B = NARROW_GDN_A + 2 * GDN_HEADS


def _mm(a, b):
    return jnp.dot(a.astype(BF16), b.astype(BF16), preferred_element_type=F32)


def _mm_nt(a, b):
    return lax.dot_general(a.astype(BF16), b.astype(BF16), (((1,), (1,)), ((), ())), preferred_element_type=F32)


def _mm_tn(a, b):
    return lax.dot_general(a.astype(BF16), b.astype(BF16), (((0,), (0,)), ((), ())), preferred_element_type=F32)


def _unit_triangular_inverses(lmats, row, col):
    q = lmats[0].shape[0]
    eye = (row == col).astype(F32)

    def same_block(shift):
        return (row >> shift) == (col >> shift)

    blk = same_block(3)
    ds = [jnp.where(blk, m, 0.0) for m in lmats]
    d2s = [_mm(d, d) for d in ds]
    d4s = [_mm(d2, d2) for d2 in d2s]
    xs = [_mm(eye - d, eye + d2) for d, d2 in zip(ds, d2s)]
    xs = [_mm(x, eye + d4) for x, d4 in zip(xs, d4s)]
    shift = 4
    while (1 << shift) <= q:
        level = same_block(shift) & jnp.logical_not(same_block(shift - 1))
        xes = [_mm(x, jnp.where(level, m, 0.0)) for x, m in zip(xs, lmats)]
        xs = [x - _mm(xe, x) for x, xe in zip(xs, xes)]
        shift += 1
    return xs


def _gdn_chunk_operands(q_ref, k_ref, v_ref, n_ref, ap_ref, bias_ref, direction):
    cq = q_ref.shape[0]
    row = lax.broadcasted_iota(jnp.int32, (cq, cq), 0)
    col = lax.broadcasted_iota(jnp.int32, (cq, cq), 1)
    if direction == 0:
        incl, strict = col <= row, col < row
    else:
        incl, strict = col >= row, col > row
    nar = n_ref[...]
    g_all = ap_ref[...] * jax.nn.softplus(nar + bias_ref[...])
    beta_all = jax.nn.sigmoid(nar)
    cum_op = jnp.concatenate([incl.astype(F32), jnp.ones((cq, cq), F32)], axis=0)
    cums = jnp.dot(cum_op, g_all, preferred_element_type=F32, precision=lax.Precision.HIGHEST)
    gc_all, tot_all = cums[:cq], cums[cq:]
    gc_t = gc_all.T
    egc_all = jnp.exp(gc_all)
    erem_all = jnp.exp(tot_all - gc_all)
    etot_all = jnp.exp(tot_all)
    heads = range(GDN_HEADS)
    la = [NARROW_GDN_A + direction * GDN_HEADS + h for h in heads]
    lb = [NARROW_GDN_B + direction * GDN_HEADS + h for h in heads]
    hs = [slice(h * GDN_HEAD_DIM, (h + 1) * GDN_HEAD_DIM) for h in heads]
    q = [q_ref[:, hs[h]] for h in heads]
    k = [k_ref[:, hs[h]] for h in heads]
    v = [v_ref[:, hs[h]] for h in heads]
    beta = [beta_all[:, lb[h]:lb[h] + 1] for h in heads]
    egc = [egc_all[:, la[h]:la[h] + 1] for h in heads]
    decay = [jnp.exp(jnp.where(incl, gc_all[:, la[h]:la[h] + 1] - gc_t[la[h]:la[h] + 1, :], -1e30)) for h in heads]
    kb = [k[h] * beta[h] for h in heads]
    return dict(
        q=q, k=k, decay=decay, strict=[strict] * GDN_HEADS,
        qe=[q[h] * egc[h] for h in heads],
        rhs=[jnp.concatenate([v[h] * beta[h], kb[h] * egc[h]], axis=-1) for h in heads],
        kb=kb,
        ke=[k[h] * erem_all[:, la[h]:la[h] + 1] for h in heads],
        etot=[etot_all[0:1, la[h]:la[h] + 1] for h in heads],
    )


def _gdn_scan_kernel(qf, kf, vf, nf, qb, kb, vb, nb, ap_ref, bias_ref, of_ref, ob_ref, s_ref):
    @pl.when(pl.program_id(1) == 0)
    def _():
        s_ref[...] = jnp.zeros_like(s_ref)

    cq = qf.shape[0]
    row = lax.broadcasted_iota(jnp.int32, (cq, cq), 0)
    col = lax.broadcasted_iota(jnp.int32, (cq, cq), 1)
    fwd = _gdn_chunk_operands(qf, kf, vf, nf, ap_ref, bias_ref, 0)
    bwd = _gdn_chunk_operands(qb, kb, vb, nb, ap_ref, bias_ref, 1)
    c = {name: fwd[name] + bwd[name] for name in fwd}
    n = range(2 * GDN_HEADS)
    s = [s_ref[i] for i in n]
    kk = [_mm_nt(c['kb'][i], c['k'][i]) for i in n]
    qk = [_mm_nt(c['q'][i], c['k'][i]) for i in n]
    lmats = [jnp.where(c['strict'][i], kk[i] * c['decay'][i], 0.0) for i in n]
    t_inv = _unit_triangular_inverses(lmats, row, col)
    uw = [_mm(t_inv[i], c['rhs'][i]) for i in n]
    ws = [_mm(uw[i][:, GDN_HEAD_DIM:], s[i]) for i in n]
    qs = [_mm(c['qe'][i], s[i]) for i in n]
    v_new = [uw[i][:, :GDN_HEAD_DIM] - ws[i] for i in n]
    outs = [qs[i] + _mm(qk[i] * c['decay'][i], v_new[i]) for i in n]
    upd = [_mm_tn(c['ke'][i], v_new[i]) for i in n]
    of_ref[...] = jnp.concatenate(outs[:GDN_HEADS], axis=-1)
    ob_ref[...] = jnp.concatenate(outs[GDN_HEADS:], axis=-1)
    for i in n:
        s_ref[i] = s[i] * c['etot'][i] + upd[i]


def gdn_scan(qkv, u_narrow, a_par, bias_par, group, carried, *, cq=GDN_CHUNK):
    t = qkv.shape[0]
    w = GDN_WIDTH
    row0, batch, seq = group
    nc = seq // cq
    base = row0 // cq

    def fwd(col):
        return lambda b, c: (base + b * nc + c, col)

    def bwd(col):
        return lambda b, c: (base + b * nc + nc - 1 - c, col)

    wide = lambda idx: pl.BlockSpec((cq, w), idx)
    narrow = lambda idx: pl.BlockSpec((cq, LANES), idx)
    par = pl.BlockSpec((1, LANES), lambda b, c: (0, 0))
    return _group_call(
        _gdn_scan_kernel, "gdn_scan", (batch, nc),
        [wide(fwd(0)), wide(fwd(1)), wide(fwd(2)), narrow(fwd(0)),
         wide(bwd(0)), wide(bwd(1)), wide(bwd(2)), narrow(bwd(0)), par, par],
        (qkv, qkv, qkv, u_narrow, qkv, qkv, qkv, u_narrow, a_par, bias_par),
        [wide(fwd(0)), wide(bwd(0))],
        [jax.ShapeDtypeStruct((t, w), F32), jax.ShapeDtypeStruct((t, w), F32)],
        [pltpu.VMEM((2 * GDN_HEADS, GDN_HEAD_DIM, GDN_HEAD_DIM), F32)], carried)


def _gdn_gate_kernel(of_ref, ob_ref, z_ref, w_ref, o_ref):
    for h in range(GDN_HEADS):
        hs = slice(h * GDN_HEAD_DIM, (h + 1) * GDN_HEAD_DIM)
        o = _rms_normalize(of_ref[:, hs] + ob_ref[:, hs], w_ref[...])
        z = z_ref[:, hs]
        o_ref[:, hs] = o * (z * jax.nn.sigmoid(z))


def gdn_gate(o_f, o_b, u_wide, norm_w, *, tm=512):
    t, w = o_f.shape
    zblk = U_GDN_Z // GDN_WIDTH
    row = lambda i: (i, 0)
    return pl.pallas_call(
        _gdn_gate_kernel,
        grid=(t // tm,),
        in_specs=[pl.BlockSpec((tm, w), row), pl.BlockSpec((tm, w), row), pl.BlockSpec((tm, w), lambda i: (i, zblk)),
                  pl.BlockSpec((1, GDN_HEAD_DIM), lambda i: (0, 0))],
        out_specs=pl.BlockSpec((tm, w), row),
        out_shape=jax.ShapeDtypeStruct((t, w), F32),
        compiler_params=_params(("parallel",)),
        name="gdn_gate",
    )(o_f, o_b, u_wide, norm_w.reshape(1, -1))


def gdn_mixer(u_wide, u_narrow, groups, conv_w, dt_bias, a_log, norm_w):
    ones, zeros = jnp.ones((GDN_WIDTH,), F32), jnp.zeros((GDN_WIDTH,), F32)
    l2 = jnp.stack([jnp.concatenate([ones, ones, zeros]),
                    jnp.concatenate([ones * GDN_HEAD_DIM ** -0.5, ones, zeros])])
    qkv = conv_act(u_wide, U_GDN_QKV, GDN_CONV_DIM, conv_w, jnp.zeros((GDN_CONV_DIM,), F32), groups,
                   l2=l2, head_dim=GDN_HEAD_DIM)
    lanes = slice(NARROW_GDN_A, NARROW_GDN_B)
    a_par = jnp.zeros((1, LANES), F32).at[0, lanes].set(-jnp.exp(a_log).reshape(-1))
    bias_par = jnp.zeros((1, LANES), F32).at[0, lanes].set(dt_bias.reshape(-1))
    outs = None
    for group in groups:
        outs = gdn_scan(qkv, u_narrow, a_par, bias_par, group, outs)
    return gdn_gate(outs[0], outs[1], u_wide, norm_w)


SSD_BC = SSD_GROUPS * SSD_STATE
SSD_REP = SSD_HEADS // SSD_GROUPS


def _ssd_chunk_operands(xbc_ref, n_ref, ap_ref, bias_ref, direction):
    cq = xbc_ref.shape[0]
    row = lax.broadcasted_iota(jnp.int32, (cq, cq), 0)
    col = lax.broadcasted_iota(jnp.int32, (cq, cq), 1)
    incl = col <= row if direction == 0 else col >= row
    dt_all = jax.nn.softplus(n_ref[...] + bias_ref[...])
    cum_op = jnp.concatenate([incl.astype(F32), jnp.ones((cq, cq), F32)], axis=0)
    cums = jnp.dot(cum_op, dt_all * ap_ref[...], preferred_element_type=F32, precision=lax.Precision.HIGHEST)
    ac_all, tot_all = cums[:cq], cums[cq:]
    ac_t = ac_all.T
    eac_all = jnp.exp(ac_all)
    erem_all = jnp.exp(tot_all - ac_all)
    etot_all = jnp.exp(tot_all)
    groups = range(SSD_GROUPS)
    bm = [xbc_ref[:, SSD_WIDTH + g * SSD_STATE:SSD_WIDTH + (g + 1) * SSD_STATE] for g in groups]
    cm = [xbc_ref[:, SSD_WIDTH + SSD_BC + g * SSD_STATE:SSD_WIDTH + SSD_BC + (g + 1) * SSD_STATE] for g in groups]
    cb = [_mm_nt(cm[g], bm[g]) for g in groups]
    heads = range(SSD_HEADS)
    ln = [direction * SSD_HEADS + h for h in heads]
    seg = [jnp.exp(jnp.where(incl, ac_all[:, ln[h]:ln[h] + 1] - ac_t[ln[h]:ln[h] + 1, :], -1e30)) for h in heads]
    return dict(
        scores=[cb[h // SSD_REP] * seg[h] for h in heads],
        xdt=[xbc_ref[:, h * SSD_HEAD_DIM:(h + 1) * SSD_HEAD_DIM] * dt_all[:, ln[h]:ln[h] + 1] for h in heads],
        b_end=[bm[h // SSD_REP] * erem_all[:, ln[h]:ln[h] + 1] for h in heads],
        c_in=[cm[h // SSD_REP] * eac_all[:, ln[h]:ln[h] + 1] for h in heads],
        etot=[etot_all[0:1, ln[h]:ln[h] + 1] for h in heads],
    )


def _ssd_scan_kernel(xf, nf, xb, nb, ap_ref, bias_ref, of_ref, ob_ref, s_ref):
    @pl.when(pl.program_id(1) == 0)
    def _():
        s_ref[...] = jnp.zeros_like(s_ref)

    fwd = _ssd_chunk_operands(xf, nf, ap_ref, bias_ref, 0)
    bwd = _ssd_chunk_operands(xb, nb, ap_ref, bias_ref, 1)
    c = {name: fwd[name] + bwd[name] for name in fwd}
    n = range(2 * SSD_HEADS)
    s = [s_ref[i] for i in n]
    y = [_mm(c['scores'][i], c['xdt'][i]) + _mm(c['c_in'][i], s[i]) for i in n]
    upd = [_mm_tn(c['b_end'][i], c['xdt'][i]) for i in n]
    of_ref[...] = jnp.concatenate(y[:SSD_HEADS], axis=-1)
    ob_ref[...] = jnp.concatenate(y[SSD_HEADS:], axis=-1)
    for i in n:
        s_ref[i] = s[i] * c['etot'][i] + upd[i]


def ssd_scan(xbc, u_narrow, a_par, bias_par, group, carried, *, cq=SSD_CHUNK):
    t = xbc.shape[0]
    row0, batch, seq = group
    nc = seq // cq
    base = row0 // cq

    def fwd(b, c):
        return (base + b * nc + c, 0)

    def bwd(b, c):
        return (base + b * nc + nc - 1 - c, 0)

    wide = lambda idx: pl.BlockSpec((cq, SSD_CONV_DIM), idx)
    narrow = lambda idx: pl.BlockSpec((cq, LANES), idx)
    out = lambda idx: pl.BlockSpec((cq, SSD_WIDTH), idx)
    par = pl.BlockSpec((1, LANES), lambda b, c: (0, 0))
    return _group_call(
        _ssd_scan_kernel, "ssd_scan", (batch, nc),
        [wide(fwd), narrow(fwd), wide(bwd), narrow(bwd), par, par],
        (xbc, u_narrow, xbc, u_narrow, a_par, bias_par),
        [out(fwd), out(bwd)],
        [jax.ShapeDtypeStruct((t, SSD_WIDTH), F32), jax.ShapeDtypeStruct((t, SSD_WIDTH), F32)],
        [pltpu.VMEM((2 * SSD_HEADS, SSD_STATE, SSD_HEAD_DIM), F32)], carried)


def _ssd_gate_kernel(yf_ref, yb_ref, x_ref, z_ref, d_ref, w_ref, o_ref):
    z = z_ref[...]
    y = (yf_ref[...] + yb_ref[...] + x_ref[...] * d_ref[...]) * (z * jax.nn.sigmoid(z))
    o_ref[...] = _rms_normalize(y, w_ref[...])


def ssd_gate(y_f, y_b, xbc, u_wide, d_wide, norm_w, *, tm=512):
    t, w = y_f.shape
    row = lambda i: (i, 0)
    vec = pl.BlockSpec((1, w), lambda i: (0, 0))
    return pl.pallas_call(
        _ssd_gate_kernel,
        grid=(t // tm,),
        in_specs=[pl.BlockSpec((tm, w), row), pl.BlockSpec((tm, w), row), pl.BlockSpec((tm, w), row),
                  pl.BlockSpec((tm, w), lambda i: (i, U_SSD_Z // SSD_WIDTH)), vec, vec],
        out_specs=pl.BlockSpec((tm, w), row),
        out_shape=jax.ShapeDtypeStruct((t, w), F32),
        compiler_params=_params(("parallel",)),
        name="ssd_gate",
    )(y_f, y_b, xbc, u_wide, d_wide, norm_w.reshape(1, -1))


def ssd_mixer(u_wide, u_narrow, groups, conv_w, conv_b, dt_bias, a_log, d_skip, norm_w):
    xbc = conv_act(u_wide, U_SSD_XBC, SSD_CONV_DIM, conv_w, conv_b, groups)
    lanes = slice(0, 2 * SSD_HEADS)
    a_par = jnp.zeros((1, LANES), F32).at[0, lanes].set(-jnp.exp(a_log).reshape(-1))
    bias_par = jnp.zeros((1, LANES), F32).at[0, lanes].set(dt_bias.reshape(-1))
    outs = None
    for group in groups:
        outs = ssd_scan(xbc, u_narrow, a_par, bias_par, group, outs)
    d_wide = jnp.repeat(d_skip, SSD_HEAD_DIM).reshape(1, SSD_WIDTH)
    return ssd_gate(outs[0], outs[1], xbc, u_wide, d_wide, norm_w)


FNET_N2 = 128
FNET_PAIR = 2 * FNET_HEAD_DIM
FNET_SLAB = 16


def _dft_cos_sin(n, scale):
    idx = jnp.arange(n, dtype=jnp.int32)
    ang = (2.0 * math.pi / n) * ((idx[:, None] * idx[None, :]) % n).astype(F32)
    return jnp.cos(ang) * scale, jnp.sin(ang) * scale


def _fnet_channel_kernel(u_ref, m_ref, y_ref):
    for h in range(FNET_HEADS):
        uh = u_ref[:, h * FNET_HEAD_DIM:(h + 1) * FNET_HEAD_DIM].astype(BF16)
        y_ref[h] = jnp.dot(uh, m_ref[...], preferred_element_type=F32).astype(BF16)


def _re_im_products(c_ref, s_ref, x):
    p = jnp.dot(c_ref[...], x, preferred_element_type=F32)
    q = jnp.dot(s_ref[...], x, preferred_element_type=F32)
    res = []
    for j in range(x.shape[1] // FNET_PAIR):
        re = slice(j * FNET_PAIR, j * FNET_PAIR + FNET_HEAD_DIM)
        im = slice(j * FNET_PAIR + FNET_HEAD_DIM, (j + 1) * FNET_PAIR)
        res.append((p[:, re] + q[:, im], p[:, im] - q[:, re]))
    return res


def _fnet_stage1_kernel(x_ref, c_ref, s_ref, tc_ref, ts_ref, o_ref):
    for j, (ar, ai) in enumerate(_re_im_products(c_ref, s_ref, x_ref[0])):
        lanes = slice(j * FNET_HEAD_DIM, (j + 1) * FNET_HEAD_DIM)
        tc, ts = tc_ref[:, lanes], ts_ref[:, lanes]
        o_ref[0, j] = jnp.concatenate([ar * tc + ai * ts, ai * tc - ar * ts], axis=-1).astype(BF16)


def _fnet_stage2_kernel(x_ref, c_ref, s_ref, o_ref):
    o_ref[0] = jnp.concatenate([re for re, _ in _re_im_products(c_ref, s_ref, x_ref[0])], axis=-1)


def _fnet_out_kernel(f_ref, w_ref, b_ref, o_ref):
    acc = b_ref[...]
    for h in range(FNET_HEADS):
        acc = acc + jnp.dot(f_ref[h].astype(BF16), w_ref[h * FNET_HEAD_DIM:(h + 1) * FNET_HEAD_DIM, :],
                            preferred_element_type=F32)
    o_ref[...] = acc


def _fnet_sequence_dft(y, batch, seq, *, slab=FNET_SLAB):
    t = y.shape[1]
    n1, n2 = seq // FNET_N2, FNET_N2
    hb = FNET_HEADS * batch
    c1, s1 = (m.astype(BF16) for m in _dft_cos_sin(n1, n1 ** -0.5))
    k1 = jnp.arange(n1, dtype=jnp.int32)[:, None]
    m2 = jnp.arange(n2, dtype=jnp.int32)[None, :]
    ang = (2.0 * math.pi / seq) * ((k1 * m2) % seq).astype(F32)
    tw_c = jnp.repeat(jnp.cos(ang), FNET_HEAD_DIM, axis=1)
    tw_s = jnp.repeat(jnp.sin(ang), FNET_HEAD_DIM, axis=1)
    mat1 = pl.BlockSpec((n1, n1), lambda g, j: (0, 0))
    a = pl.pallas_call(
        _fnet_stage1_kernel,
        grid=(hb, n2 // slab),
        in_specs=[pl.BlockSpec((1, n1, slab * FNET_PAIR), lambda g, j: (g, 0, j)), mat1, mat1,
                  pl.BlockSpec((n1, slab * FNET_HEAD_DIM), lambda g, j: (0, j)),
                  pl.BlockSpec((n1, slab * FNET_HEAD_DIM), lambda g, j: (0, j))],
        out_specs=pl.BlockSpec((1, slab, n1, FNET_PAIR), lambda g, j: (g, j, 0, 0)),
        out_shape=jax.ShapeDtypeStruct((hb, n2, n1, FNET_PAIR), BF16),
        compiler_params=_params(("parallel", "parallel")),
        name="fnet_stage1",
    )(y.reshape(hb, n1, n2 * FNET_PAIR), c1, s1, tw_c, tw_s)

    c2, s2 = (m.astype(BF16) for m in _dft_cos_sin(n2, n2 ** -0.5))
    kslab = min(slab, n1)
    mat2 = pl.BlockSpec((n2, n2), lambda g, j: (0, 0))
    f = pl.pallas_call(
        _fnet_stage2_kernel,
        grid=(hb, n1 // kslab),
        in_specs=[pl.BlockSpec((1, n2, kslab * FNET_PAIR), lambda g, j: (g, 0, j)), mat2, mat2],
        out_specs=pl.BlockSpec((1, n2, kslab * FNET_HEAD_DIM), lambda g, j: (g, 0, j)),
        out_shape=jax.ShapeDtypeStruct((hb, n2, n1 * FNET_HEAD_DIM), F32),
        compiler_params=_params(("parallel", "parallel")),
        name="fnet_stage2",
    )(a.reshape(hb, n2, n1 * FNET_PAIR), c2, s2)
    return f.reshape(FNET_HEADS, t, FNET_HEAD_DIM)


def fourier_mixer(u_wide, groups, w, b, *, tm=512):
    t = u_wide.shape[0]
    cc, sc = _dft_cos_sin(FNET_HEAD_DIM, FNET_HEAD_DIM ** -0.5)
    y = pl.pallas_call(
        _fnet_channel_kernel,
        grid=(t // tm,),
        in_specs=[pl.BlockSpec((tm, FNET_WIDTH), lambda i: (i, U_FNET // FNET_WIDTH)),
                  pl.BlockSpec((FNET_HEAD_DIM, FNET_PAIR), lambda i: (0, 0))],
        out_specs=pl.BlockSpec((FNET_HEADS, tm, FNET_PAIR), lambda i: (0, i, 0)),
        out_shape=jax.ShapeDtypeStruct((FNET_HEADS, t, FNET_PAIR), BF16),
        compiler_params=_params(("parallel",)),
        name="fnet_channel",
    )(u_wide, jnp.concatenate([cc, -sc], axis=-1).astype(BF16))
    f = jnp.concatenate([_fnet_sequence_dft(y[:, row0:row0 + batch * seq], batch, seq) for row0, batch, seq in groups],
                        axis=1)
    return pl.pallas_call(
        _fnet_out_kernel,
        grid=(t // tm,),
        in_specs=[pl.BlockSpec((FNET_HEADS, tm, FNET_HEAD_DIM), lambda i: (0, i, 0)),
                  pl.BlockSpec((FNET_WIDTH, FNET_WIDTH), lambda i: (0, 0)),
                  pl.BlockSpec((1, FNET_WIDTH), lambda i: (0, 0))],
        out_specs=pl.BlockSpec((tm, FNET_WIDTH), lambda i: (i, 0)),
        out_shape=jax.ShapeDtypeStruct((t, FNET_WIDTH), F32),
        compiler_params=_params(("parallel",)),
        name="fnet_out",
    )(f, w.astype(BF16), b.reshape(1, -1))


S5_TILE = 256
S5_NSTATE = S5_GROUPS * S5_STATE
S5_COLS = 512
S5_BLOCKS = S5_WIDTH // LANES


def _s5_scan_kernel(u_ref, pi_ref, pit_ref, b_ref, c_ref, lre_ref, lim_ref, o_ref,
                    sre, sim, pre, pim, car_re, car_im, e_re, e_im, cin_re, cin_im):
    tl = u_ref.shape[0]
    steps = tl // 8
    first_tile = pl.program_id(2) == 0

    def col(j):
        return slice(j * S5_COLS, (j + 1) * S5_COLS)

    def rows(t):
        return pl.ds(pl.multiple_of(t * 8, 8), 8)

    def bcast(ref, j):
        return jnp.broadcast_to(ref[0, :, col(j)], (8, S5_COLS))

    @pl.when(first_tile)
    def _():
        car_re[...] = jnp.zeros_like(car_re)
        car_im[...] = jnp.zeros_like(car_im)
        for j in range(S5_BLOCKS):
            lr, li = bcast(lre_ref, j), bcast(lim_ref, j)

            def power_step(t, p, j=j, lr=lr, li=li):
                pr, pi_ = p
                pre[rows(t), col(j)] = pr
                pim[rows(t), col(j)] = pi_
                return pr * lr - pi_ * li, pr * li + pi_ * lr

            lax.fori_loop(0, steps, power_step, (lr, li))

    up = jnp.dot(pi_ref[0], u_ref[...].astype(BF16), preferred_element_type=F32).astype(BF16)
    for j in range(S5_BLOCKS):
        uj = up[:, j * LANES:(j + 1) * LANES]
        sre[:, col(j)] = jnp.dot(uj, b_ref[0, j, 0], preferred_element_type=F32)
        sim[:, col(j)] = jnp.dot(uj, b_ref[0, j, 1], preferred_element_type=F32)

    for j in range(S5_BLOCKS):
        lr, li = bcast(lre_ref, j), bcast(lim_ref, j)

        def scan_step(t, s, j=j, lr=lr, li=li):
            sr, si = s
            nr = lr * sr - li * si + sre[rows(t), col(j)]
            ni = lr * si + li * sr + sim[rows(t), col(j)]
            sre[rows(t), col(j)] = nr
            sim[rows(t), col(j)] = ni
            return nr, ni

        zero = jnp.zeros((8, S5_COLS), F32)
        er, ei = lax.fori_loop(0, steps, scan_step, (zero, zero), unroll=4)
        e_re[:, col(j)] = er
        e_im[:, col(j)] = ei

    last = pl.ds((steps - 1) * 8, 1)
    lsr, lsi = pre[last, :], pim[last, :]
    cr, ci = car_re[...], car_im[...]
    for r in range(8):
        cin_re[r:r + 1, :] = cr
        cin_im[r:r + 1, :] = ci
        cr, ci = (e_re[r:r + 1, :] + lsr * cr - lsi * ci, e_im[r:r + 1, :] + lsr * ci + lsi * cr)
    car_re[...] = cr
    car_im[...] = ci

    for j in range(S5_BLOCKS):
        cr, ci = cin_re[:, col(j)], cin_im[:, col(j)]

        def fix_step(t, carry, j=j, cr=cr, ci=ci):
            pr, pi_ = pre[rows(t), col(j)], pim[rows(t), col(j)]
            sre[rows(t), col(j)] = sre[rows(t), col(j)] + pr * cr - pi_ * ci
            sim[rows(t), col(j)] = sim[rows(t), col(j)] + pr * ci + pi_ * cr
            return carry

        lax.fori_loop(0, steps, fix_step, 0, unroll=4)

    ys = []
    for j in range(S5_BLOCKS):
        ys.append(jnp.dot(sre[:, col(j)].astype(BF16), c_ref[0, j, 0], preferred_element_type=F32)
                  + jnp.dot(sim[:, col(j)].astype(BF16), c_ref[0, j, 1], preferred_element_type=F32))
    y = jnp.concatenate(ys, axis=-1)
    y_hi = y.astype(BF16)
    y_lo = (y - y_hi.astype(F32)).astype(BF16)
    pit = pit_ref[0]
    o_ref[0] = jnp.dot(pit, y_hi, preferred_element_type=F32) + jnp.dot(pit, y_lo, preferred_element_type=F32)


def _s5_operators(lam_re, lam_im, log_step, b_re, b_im, c_re, c_im, tl):
    lam = lax.complex(lam_re, lam_im)
    delta = jnp.exp(log_step)[..., None]
    lam_bar = jnp.exp(lam * delta)
    b_scale = (lam_bar - 1.0) / lam
    eye = jnp.eye(8, dtype=F32)
    gpb = S5_GROUPS // S5_BLOCKS
    bd = lax.complex(b_re, b_im)[None] * b_scale[..., None]
    bd = bd.reshape(2, S5_BLOCKS, gpb, S5_STATE, S5_GROUP_CH)
    bmat = jnp.stack([jnp.einsum('djgpc,gh->djgchp', part, eye).reshape(2, S5_BLOCKS, LANES, S5_COLS)
                      for part in (bd.real, bd.imag)], axis=2).astype(BF16)
    cd = lax.complex(c_re, c_im).reshape(2, S5_BLOCKS, gpb, S5_GROUP_CH, S5_STATE)
    cmat = jnp.stack([jnp.einsum('djgcp,gh->djhpgc', part, eye).reshape(2, S5_BLOCKS, S5_COLS, LANES)
                      for part in (cd.real, -cd.imag)], axis=2).astype(BF16)
    lre = lam_bar.real.reshape(2, 1, S5_NSTATE)
    lim = lam_bar.imag.reshape(2, 1, S5_NSTATE)
    steps = tl // 8
    rt = jnp.arange(tl)
    src = (rt % 8) * steps + rt // 8
    fwd = (src[:, None] == rt[None, :])
    bwd = ((tl - 1 - src)[:, None] == rt[None, :])
    pi = jnp.stack([fwd, bwd]).astype(BF16)
    return pi, jnp.swapaxes(pi, 1, 2), bmat, cmat, lre, lim


def s5_scan(u_wide, ops, group, carried, *, tl=S5_TILE):
    pi, pit, bmat, cmat, lre, lim = ops
    t = u_wide.shape[0]
    row0, batch, seq = group
    nt = seq // tl
    base = row0 // tl
    ublk = U_S5 // S5_WIDTH

    def tile_idx(b, d, i):
        return base + b * nt + jnp.where(d == 0, i, nt - 1 - i)

    state = pltpu.VMEM((tl, S5_NSTATE), F32)
    row = pltpu.VMEM((1, S5_NSTATE), F32)
    seg = pltpu.VMEM((8, S5_NSTATE), F32)
    return _group_call(
        _s5_scan_kernel, "s5_scan", (batch, 2, nt),
        [
            pl.BlockSpec((tl, S5_WIDTH), lambda b, d, i: (tile_idx(b, d, i), ublk)),
            pl.BlockSpec((1, tl, tl), lambda b, d, i: (d, 0, 0)),
            pl.BlockSpec((1, tl, tl), lambda b, d, i: (d, 0, 0)),
            pl.BlockSpec((1, S5_BLOCKS, 2, LANES, S5_COLS), lambda b, d, i: (d, 0, 0, 0, 0)),
            pl.BlockSpec((1, S5_BLOCKS, 2, S5_COLS, LANES), lambda b, d, i: (d, 0, 0, 0, 0)),
            pl.BlockSpec((1, 1, S5_NSTATE), lambda b, d, i: (d, 0, 0)),
            pl.BlockSpec((1, 1, S5_NSTATE), lambda b, d, i: (d, 0, 0)),
        ],
        (u_wide, pi, pit, bmat, cmat, lre, lim),
        [pl.BlockSpec((1, tl, S5_WIDTH), lambda b, d, i: (d, tile_idx(b, d, i), 0))],
        [jax.ShapeDtypeStruct((2, t, S5_WIDTH), F32)],
        [state, state, state, state, row, row, seg, seg, seg, seg], carried)[0]


def _s5_gate_kernel(y_ref, u_ref, d_ref, w_ref, b_ref, o_ref):
    y = y_ref[0] + y_ref[1] + u_ref[...] * d_ref[...]
    h = jax.nn.gelu(y)
    gate = jnp.dot(h.astype(BF16), w_ref[...], preferred_element_type=F32) + b_ref[...]
    o_ref[...] = h * jax.nn.sigmoid(gate)


def s5_gate(y2, u_wide, d_skip, glu_w, glu_b, *, tm=512):
    t = u_wide.shape[0]
    ublk = U_S5 // S5_WIDTH
    return pl.pallas_call(
        _s5_gate_kernel,
        grid=(t // tm,),
        in_specs=[
            pl.BlockSpec((2, tm, S5_WIDTH), lambda i: (0, i, 0)),
            pl.BlockSpec((tm, S5_WIDTH), lambda i: (i, ublk)),
            pl.BlockSpec((1, S5_WIDTH), lambda i: (0, 0)),
            pl.BlockSpec((S5_WIDTH, S5_WIDTH), lambda i: (0, 0)),
            pl.BlockSpec((1, S5_WIDTH), lambda i: (0, 0)),
        ],
        out_specs=pl.BlockSpec((tm, S5_WIDTH), lambda i: (i, 0)),
        out_shape=jax.ShapeDtypeStruct((t, S5_WIDTH), F32),
        compiler_params=_params(("parallel",)),
        name="s5_gate",
    )(y2, u_wide, d_skip.reshape(1, -1), glu_w.astype(BF16), glu_b.reshape(1, -1))


def s5_mixer(u_wide, groups, ops, d_skip, glu_w, glu_b):
    y2 = None
    for group in groups:
        y2 = s5_scan(u_wide, ops, group, None if y2 is None else (y2,))
    return s5_gate(y2, u_wide, d_skip, glu_w, glu_b)


def mixers(u_wide, u_narrow, groups, p, i):
    y_ssd = ssd_mixer(u_wide, u_narrow, groups, p['ssd_conv_w'][i], p['ssd_conv_b'][i], p['ssd_dt_bias'][i],
                      p['ssd_a_log'][i], p['ssd_d'][i], p['ssd_norm'][i])
    y_gdn = gdn_mixer(u_wide, u_narrow, groups, p['gdn_conv_w'][i], p['gdn_dt_bias'][i], p['gdn_a_log'][i],
                      p['gdn_norm'][i])
    y_fnet = fourier_mixer(u_wide, groups, p['fnet_w'][i], p['fnet_b'][i])
    s5_ops = _s5_operators(p['s5_lambda_re'][i], p['s5_lambda_im'][i], p['s5_log_step'][i], p['s5_b_re'][i],
                           p['s5_b_im'][i], p['s5_c_re'][i], p['s5_c_im'][i], S5_TILE)
    y_s5 = s5_mixer(u_wide, groups, s5_ops, p['s5_d'][i], p['s5_glu_w'][i], p['s5_glu_b'][i])
    return y_ssd, y_gdn, y_fnet, y_s5


def _split_w_in(w_in):
    o1 = SSD_IN
    o2 = o1 + GDN_IN
    ssd_z = w_in[..., :SSD_WIDTH]
    ssd_xbc = w_in[..., SSD_WIDTH:SSD_WIDTH + SSD_CONV_DIM]
    ssd_dt = w_in[..., SSD_WIDTH + SSD_CONV_DIM:o1]
    gdn_qkv = w_in[..., o1:o1 + GDN_CONV_DIM]
    gdn_z = w_in[..., o1 + GDN_CONV_DIM:o1 + 4 * GDN_WIDTH]
    gdn_ab = w_in[..., o1 + 4 * GDN_WIDTH:o2]
    rest = w_in[..., o2:]
    wide = jnp.concatenate([ssd_z, ssd_xbc, gdn_qkv, gdn_z, rest], axis=-1).astype(BF16)
    pad = jnp.zeros(w_in.shape[:-1] + (LANES - NARROW,), w_in.dtype)
    narrow = jnp.concatenate([ssd_dt, gdn_ab, pad], axis=-1).astype(BF16)
    return wide, narrow


def kernel(x_prompt, x_sample, norm_mix, w_in, ssd_conv_w, ssd_conv_b, ssd_dt_bias, ssd_a_log, ssd_d, ssd_norm,
           gdn_conv_w, gdn_dt_bias, gdn_a_log, gdn_norm, fnet_w, fnet_b, s5_lambda_re, s5_lambda_im, s5_log_step,
           s5_b_re, s5_b_im, s5_c_re, s5_c_im, s5_d, s5_glu_w, s5_glu_b, w_out, norm_ffn, ffn_w_gate, ffn_w_up,
           ffn_w_down, router_w, moe_w_gate, moe_w_up, moe_w_down, norm_final):
    p = dict(ssd_conv_w=ssd_conv_w, ssd_conv_b=ssd_conv_b, ssd_dt_bias=ssd_dt_bias, ssd_a_log=ssd_a_log, ssd_d=ssd_d,
             ssd_norm=ssd_norm, gdn_conv_w=gdn_conv_w, gdn_dt_bias=gdn_dt_bias, gdn_a_log=gdn_a_log,
             gdn_norm=gdn_norm, fnet_w=fnet_w, fnet_b=fnet_b, s5_lambda_re=s5_lambda_re, s5_lambda_im=s5_lambda_im,
             s5_log_step=s5_log_step, s5_b_re=s5_b_re, s5_b_im=s5_b_im, s5_c_re=s5_c_re, s5_c_im=s5_c_im, s5_d=s5_d,
             s5_glu_w=s5_glu_w, s5_glu_b=s5_glu_b)
    bp, lp, d = x_prompt.shape
    bs, ls, _ = x_sample.shape
    tp = bp * lp
    x = jnp.concatenate([x_prompt.reshape(tp, d), x_sample.reshape(bs * ls, d)], axis=0)
    groups = ((0, bp, lp), (tp, bs, ls))
    w_wide, w_narrow = _split_w_in(w_in)
    w_out_b = w_out.astype(BF16)
    ffn_w_gate, ffn_w_up, ffn_w_down = (w.astype(BF16) for w in (ffn_w_gate, ffn_w_up, ffn_w_down))
    depth = w_in.shape[0]
    for i in range(depth):
        u_wide, u_narrow = norm_inproj(x, norm_mix[i], w_wide[i], w_narrow[i])
        x = outproj_residual(mixers(u_wide, u_narrow, groups, p, i), w_out_b[i], x)
        j = i // 2
        if i % 2 == 0:
            x = dense_ffn(x, norm_ffn[i], ffn_w_gate, ffn_w_up, ffn_w_down, j)
        else:
            x = moe_block(x, norm_ffn[i], router_w[j], moe_w_gate, moe_w_up, moe_w_down, j)
    y = final_norm(x, norm_final)
    return y[:tp].reshape(bp, lp, d), y[tp:].reshape(bs, ls, d)
```

```python
import functools
import math

import jax
import jax.numpy as jnp
from jax import lax
from jax.experimental import pallas as pl
from jax.experimental.pallas import tpu as pltpu

F32 = jnp.float32
BF16 = jnp.bfloat16

D_MODEL = 2048
DEPTH = 4
GROUP_WIDTH = 512

SSD_WIDTH = GROUP_WIDTH
SSD_HEAD_DIM = 64
SSD_HEADS = SSD_WIDTH // SSD_HEAD_DIM
SSD_GROUPS = 2
SSD_STATE = 128
SSD_CHUNK = 128
SSD_CONV_DIM = SSD_WIDTH + 2 * SSD_GROUPS * SSD_STATE

GDN_WIDTH = GROUP_WIDTH
GDN_HEAD_DIM = 128
GDN_HEADS = GDN_WIDTH // GDN_HEAD_DIM
GDN_CHUNK = 64
GDN_CONV_DIM = 3 * GDN_WIDTH

CONV_K = 5

FNET_WIDTH = GROUP_WIDTH
FNET_HEADS = 4
FNET_HEAD_DIM = FNET_WIDTH // FNET_HEADS

S5_WIDTH = GROUP_WIDTH
S5_GROUP_CH = 16
S5_GROUPS = S5_WIDTH // S5_GROUP_CH
S5_STATE = 64

SSD_IN = 2 * SSD_WIDTH + 2 * SSD_GROUPS * SSD_STATE + 2 * SSD_HEADS
GDN_IN = 4 * GDN_WIDTH + 4 * GDN_HEADS
IN_COLS = SSD_IN + GDN_IN + FNET_WIDTH + S5_WIDTH

N_EXPERTS = 8
TOP_K = 2
EPS = 1e-6

LANES = 128
VMEM_LIMIT_BYTES = 56 * 1024 * 1024

U_SSD_Z = 0
U_SSD_XBC = U_SSD_Z + SSD_WIDTH
U_GDN_QKV = U_SSD_XBC + SSD_CONV_DIM
U_GDN_Z = U_GDN_QKV + GDN_CONV_DIM
U_FNET = U_GDN_Z + GDN_WIDTH
U_S5 = U_FNET + FNET_WIDTH
U_WIDE = U_S5 + S5_WIDTH
NARROW = 2 * SSD_HEADS + 4 * GDN_HEADS

ROW_TILE = 1024
FF_TILE = 256


def _params(semantics):
    return pltpu.CompilerParams(dimension_semantics=semantics, vmem_limit_bytes=VMEM_LIMIT_BYTES)


def _drop_refs(kernel, start, count, *refs):
    return kernel(*refs[:start], *refs[start + count:])


def _group_call(kernel, name, grid, in_specs, args, out_specs, out_shape, scratch_shapes, carried):
    aliases = {}
    if carried is not None:
        n_in = len(args)
        kernel = functools.partial(_drop_refs, kernel, n_in, len(carried))
        in_specs = list(in_specs) + [pl.BlockSpec(memory_space=pl.ANY)] * len(carried)
        args = tuple(args) + tuple(carried)
        aliases = {n_in + k: k for k in range(len(carried))}
    return pl.pallas_call(
        kernel, grid=grid, in_specs=in_specs, out_specs=out_specs, out_shape=out_shape,
        scratch_shapes=scratch_shapes, input_output_aliases=aliases,
        compiler_params=_params(("arbitrary",) * len(grid)), name=name,
    )(*args)


def _rms_normalize(x, gain):
    ms = jnp.mean(x * x, axis=-1, keepdims=True)
    return x * lax.rsqrt(ms + EPS) * gain


def _norm_inproj_kernel(x_ref, g_ref, w_ref, wn_ref, o_ref, on_ref, h_ref):
    @pl.when(pl.program_id(1) == 0)
    def _():
        h = _rms_normalize(x_ref[...], g_ref[...]).astype(BF16)
        h_ref[...] = h
        on_ref[...] = jnp.dot(h, wn_ref[...], preferred_element_type=F32)

    o_ref[...] = jnp.dot(h_ref[...], w_ref[...], preferred_element_type=F32)


def norm_inproj(x, gain, w_wide, w_narrow, *, tm=ROW_TILE, tn=512):
    t, d = x.shape
    n = w_wide.shape[1]
    return pl.pallas_call(
        _norm_inproj_kernel,
        grid=(t // tm, n // tn),
        in_specs=[
            pl.BlockSpec((tm, d), lambda i, j: (i, 0)),
            pl.BlockSpec((1, d), lambda i, j: (0, 0)),
            pl.BlockSpec((d, tn), lambda i, j: (0, j)),
            pl.BlockSpec((d, LANES), lambda i, j: (0, 0)),
        ],
        out_specs=[
            pl.BlockSpec((tm, tn), lambda i, j: (i, j)),
            pl.BlockSpec((tm, LANES), lambda i, j: (i, 0)),
        ],
        out_shape=[jax.ShapeDtypeStruct((t, n), F32), jax.ShapeDtypeStruct((t, LANES), F32)],
        scratch_shapes=[pltpu.VMEM((tm, d), BF16)],
        compiler_params=_params(("parallel", "arbitrary")),
        name="norm_inproj",
    )(x, gain.reshape(1, d), w_wide, w_narrow)


def _outproj_kernel(*refs):
    *y_refs, w_ref, x_ref, o_ref = refs
    acc = x_ref[...]
    for m, y_ref in enumerate(y_refs):
        k = y_ref.shape[1]
        acc = acc + jnp.dot(y_ref[...].astype(BF16), w_ref[m * k:(m + 1) * k, :], preferred_element_type=F32)
    o_ref[...] = acc


def outproj_residual(ys, w, x, *, tm=512):
    t, d = x.shape
    row = lambda i: (i, 0)
    return pl.pallas_call(
        _outproj_kernel,
        grid=(t // tm,),
        in_specs=[pl.BlockSpec((tm, y.shape[1]), row) for y in ys] + [
            pl.BlockSpec(w.shape, lambda i: (0, 0)),
            pl.BlockSpec((tm, d), row),
        ],
        out_specs=pl.BlockSpec((tm, d), row),
        out_shape=jax.ShapeDtypeStruct((t, d), F32),
        compiler_params=_params(("parallel",)),
        name="outproj_residual",
    )(*ys, w, x)


def _swiglu_step(h, wg_ref, wu_ref, wd_ref, acc_ref):
    g = jnp.dot(h, wg_ref[0].astype(BF16), preferred_element_type=F32)
    u = jnp.dot(h, wu_ref[0].astype(BF16), preferred_element_type=F32)
    a = (g * jax.nn.sigmoid(g) * u).astype(BF16)
    acc_ref[...] += jnp.dot(a, wd_ref[0].astype(BF16), preferred_element_type=F32)


def _dense_ffn_kernel(x_ref, g_ref, wg_ref, wu_ref, wd_ref, o_ref, h_ref, acc_ref):
    f = pl.program_id(1)

    @pl.when(f == 0)
    def _():
        h_ref[...] = _rms_normalize(x_ref[...], g_ref[...]).astype(BF16)
        acc_ref[...] = jnp.zeros_like(acc_ref)

    _swiglu_step(h_ref[...], wg_ref, wu_ref, wd_ref, acc_ref)

    @pl.when(f == pl.num_programs(1) - 1)
    def _():
        o_ref[...] = x_ref[...] + acc_ref[...]


def dense_ffn(x, gain, wg, wu, wd, layer, *, tm=512, tf=512):
    t, d = x.shape
    ff = wg.shape[2]
    return pl.pallas_call(
        _dense_ffn_kernel,
        grid=(t // tm, ff // tf),
        in_specs=[
            pl.BlockSpec((tm, d), lambda i, f: (i, 0)),
            pl.BlockSpec((1, d), lambda i, f: (0, 0)),
            pl.BlockSpec((1, d, tf), lambda i, f: (layer, 0, f)),
            pl.BlockSpec((1, d, tf), lambda i, f: (layer, 0, f)),
            pl.BlockSpec((1, tf, d), lambda i, f: (layer, f, 0)),
        ],
        out_specs=pl.BlockSpec((tm, d), lambda i, f: (i, 0)),
        out_shape=jax.ShapeDtypeStruct((t, d), F32),
        scratch_shapes=[pltpu.VMEM((tm, d), BF16), pltpu.VMEM((tm, d), F32)],
        compiler_params=_params(("parallel", "arbitrary")),
        name="dense_ffn",
    )(x, gain.reshape(1, d), wg, wu, wd)


def _moe_ffn_kernel(te_ref, na_ref, x_ref, wg_ref, wu_ref, wd_ref, o_ref, acc_ref):
    i = pl.program_id(0)
    f = pl.program_id(1)

    @pl.when(i < na_ref[0])
    def _():
        @pl.when(f == 0)
        def _():
            acc_ref[...] = jnp.zeros_like(acc_ref)

        _swiglu_step(x_ref[...], wg_ref, wu_ref, wd_ref, acc_ref)

        @pl.when(f == pl.num_programs(1) - 1)
        def _():
            o_ref[...] = acc_ref[...]


def moe_ffn_grouped(xs, tile_expert, n_active, wg, wu, wd, layer, *, tm=ROW_TILE, tf=FF_TILE):
    p, d = xs.shape
    n_e, ff = wg.shape[1], wg.shape[3]
    n_f = ff // tf
    wg = wg.reshape(-1, d, ff)
    wu = wu.reshape(-1, d, ff)
    wd = wd.reshape(-1, ff, d)

    def row_idx(i, f, te, na):
        return (jnp.minimum(i, na[0] - 1), 0)

    def up_idx(i, f, te, na):
        live = i < na[0]
        return (layer * n_e + te[jnp.minimum(i, na[0] - 1)], 0, jnp.where(live, f, n_f - 1))

    def down_idx(i, f, te, na):
        live = i < na[0]
        return (layer * n_e + te[jnp.minimum(i, na[0] - 1)], jnp.where(live, f, n_f - 1), 0)

    grid_spec = pltpu.PrefetchScalarGridSpec(
        num_scalar_prefetch=2,
        grid=(p // tm, n_f),
        in_specs=[
            pl.BlockSpec((tm, d), row_idx),
            pl.BlockSpec((1, d, tf), up_idx),
            pl.BlockSpec((1, d, tf), up_idx),
            pl.BlockSpec((1, tf, d), down_idx),
        ],
        out_specs=pl.BlockSpec((tm, d), row_idx),
        scratch_shapes=[pltpu.VMEM((tm, d), F32)],
    )
    return pl.pallas_call(
        _moe_ffn_kernel,
        grid_spec=grid_spec,
        out_shape=jax.ShapeDtypeStruct((p, d), F32),
        compiler_params=_params(("arbitrary", "arbitrary")),
        name="moe_ffn",
    )(tile_expert, n_active, xs, wg, wu, wd)


def _norm_router_kernel(x_ref, g_ref, wr_ref, h_ref, l_ref):
    h = _rms_normalize(x_ref[...], g_ref[...])
    h_ref[...] = h.astype(BF16)
    l_ref[...] = jnp.dot(h, wr_ref[...], preferred_element_type=F32, precision=lax.Precision.HIGHEST)


def norm_router(x, gain, router_w_padded, *, tm=512):
    t, d = x.shape
    return pl.pallas_call(
        _norm_router_kernel,
        grid=(t // tm,),
        in_specs=[
            pl.BlockSpec((tm, d), lambda i: (i, 0)),
            pl.BlockSpec((1, d), lambda i: (0, 0)),
            pl.BlockSpec((d, LANES), lambda i: (0, 0)),
        ],
        out_specs=[pl.BlockSpec((tm, d), lambda i: (i, 0)), pl.BlockSpec((tm, LANES), lambda i: (i, 0))],
        out_shape=[jax.ShapeDtypeStruct((t, d), BF16), jax.ShapeDtypeStruct((t, LANES), F32)],
        compiler_params=_params(("parallel",)),
        name="norm_router",
    )(x, gain.reshape(1, d), router_w_padded)


def _final_norm_kernel(x_ref, g_ref, o_ref):
    o_ref[...] = _rms_normalize(x_ref[...], g_ref[...])


def final_norm(x, gain, *, tm=512):
    t, d = x.shape
    return pl.pallas_call(
        _final_norm_kernel,
        grid=(t // tm,),
        in_specs=[pl.BlockSpec((tm, d), lambda i: (i, 0)), pl.BlockSpec((1, d), lambda i: (0, 0))],
        out_specs=pl.BlockSpec((tm, d), lambda i: (i, 0)),
        out_shape=jax.ShapeDtypeStruct((t, d), F32),
        compiler_params=_params(("parallel",)),
        name="final_norm",
    )(x, gain.reshape(1, d))


def moe_block(x, gain, router_w, wg, wu, wd, layer, *, tm=ROW_TILE, tf=FF_TILE):
    t, d = x.shape
    n_e = router_w.shape[1]
    wr = jnp.zeros((d, LANES), F32).at[:, :n_e].set(router_w)
    h, logits = norm_router(x, gain, wr)
    top_logit, top_idx = lax.top_k(logits[:, :n_e], TOP_K)
    top_w = jax.nn.softmax(top_logit, axis=-1)

    n_assign = t * TOP_K
    n_tiles = (n_assign + n_e * (tm - 1)) // tm + 1
    flat_e = top_idx.reshape(-1).astype(jnp.int32)
    order = jnp.argsort(flat_e, stable=True).astype(jnp.int32)
    rank = jnp.argsort(order).astype(jnp.int32)
    counts = jnp.sum((flat_e[:, None] == jnp.arange(n_e, dtype=jnp.int32)[None, :]).astype(jnp.int32), axis=0)
    padded = ((counts + tm - 1) // tm) * tm
    pad_end = jnp.cumsum(padded)
    pad_start = pad_end - padded
    start = jnp.cumsum(counts) - counts
    tile_start = jnp.arange(n_tiles, dtype=jnp.int32) * tm
    tile_expert = jnp.minimum(jnp.searchsorted(pad_end, tile_start, side="right"), n_e - 1).astype(jnp.int32)
    n_active = (pad_end[-1] // tm).astype(jnp.int32).reshape(1)
    row_e = jnp.repeat(tile_expert, tm)
    within = jnp.arange(n_tiles * tm, dtype=jnp.int32) - pad_start[row_e]
    sorted_pos = jnp.clip(start[row_e] + within, 0, n_assign - 1)
    src_token = jnp.where(within < counts[row_e], order[sorted_pos] // TOP_K, 0)
    pos = (pad_start[flat_e] + rank - start[flat_e]).reshape(t, TOP_K)

    xs = jnp.take(h, src_token, axis=0)
    ys = moe_ffn_grouped(xs, tile_expert, n_active, wg, wu, wd, layer, tm=tm, tf=tf)
    return (x + top_w[:, 0:1] * jnp.take(ys, pos[:, 0], axis=0) + top_w[:, 1:2] * jnp.take(ys, pos[:, 1], axis=0))


CONV_HALO = 8
CONV_COLS = 512


def _sequence_edges(tile, tm, groups):
    first = last = tile < 0
    for row0, batch, seq in groups:
        tiles_per_seq = seq // tm
        rel = tile - row0 // tm
        inside = jnp.logical_and(rel >= 0, rel < batch * tiles_per_seq)
        pos = jnp.maximum(rel, 0) % tiles_per_seq
        first = jnp.logical_or(first, jnp.logical_and(inside, pos == 0))
        last = jnp.logical_or(last, jnp.logical_and(inside, pos == tiles_per_seq - 1))
    return first, last


def _conv_act_kernel(prev_ref, cur_ref, next_ref, w_ref, b_ref, l2_ref, o_ref, buf, *, groups, head_dim):
    tm = cur_ref.shape[0]
    first, last = _sequence_edges(pl.program_id(0), tm, groups)
    buf[0:CONV_HALO, :] = jnp.where(first, 0.0, prev_ref[...])
    buf[CONV_HALO:CONV_HALO + tm, :] = cur_ref[...]
    buf[CONV_HALO + tm:2 * CONV_HALO + tm, :] = jnp.where(last, 0.0, next_ref[...])
    acc = jnp.broadcast_to(b_ref[...], (tm, cur_ref.shape[1]))
    for k in range(CONV_K):
        acc = acc + buf[pl.ds(CONV_HALO - CONV_K // 2 + k, tm), :] * w_ref[k:k + 1, :]
    y = acc * jax.nn.sigmoid(acc)
    if head_dim is None:
        o_ref[...] = y
    else:
        for h in range(y.shape[1] // head_dim):
            hs = slice(h * head_dim, (h + 1) * head_dim)
            yh = y[:, hs]
            normed = yh * lax.rsqrt(jnp.sum(yh * yh, axis=-1, keepdims=True) + EPS) * l2_ref[1:2, hs]
            o_ref[:, hs] = jnp.where(l2_ref[0:1, hs] > 0.5, normed, yh)


def conv_act(u_wide, col0, width, conv_w, conv_b, groups, *, l2=None, head_dim=None, tm=1024):
    tm = min([tm] + [seq for _, _, seq in groups])
    t = u_wide.shape[0]
    nrow = t // tm
    nhalo = tm // CONV_HALO
    cb0 = col0 // CONV_COLS
    w = jnp.zeros((CONV_HALO, width), F32).at[:CONV_K].set(conv_w.T)
    if l2 is None:
        l2 = jnp.zeros((2, width), F32)
    l2 = jnp.zeros((CONV_HALO, width), F32).at[:2].set(l2)
    kern = functools.partial(_conv_act_kernel, groups=groups, head_dim=head_dim)
    par = pl.BlockSpec((CONV_HALO, CONV_COLS), lambda i, j: (0, j))
    return pl.pallas_call(
        kern,
        grid=(nrow, width // CONV_COLS),
        in_specs=[
            pl.BlockSpec((CONV_HALO, CONV_COLS), lambda i, j: (jnp.maximum(i * nhalo - 1, 0), cb0 + j)),
            pl.BlockSpec((tm, CONV_COLS), lambda i, j: (i, cb0 + j)),
            pl.BlockSpec((CONV_HALO, CONV_COLS), lambda i, j: (jnp.minimum((i + 1) * nhalo, t // CONV_HALO - 1), cb0 + j)),
            par,
            pl.BlockSpec((1, CONV_COLS), lambda i, j: (0, j)),
            par,
        ],
        out_specs=pl.BlockSpec((tm, CONV_COLS), lambda i, j: (i, j)),
        out_shape=jax.ShapeDtypeStruct((t, width), F32),
        scratch_shapes=[pltpu.VMEM((tm + 2 * CONV_HALO, CONV_COLS), F32)],
        compiler_params=_params(("parallel", "parallel")),
        name="conv_act",
    )(u_wide, u_wide, u_wide, w, conv_b.reshape(1, width), l2)


NARROW_GDN_A = 2 * SSD_HEADS
NARROW_GDN_B = NARROW_GDN_A + 2 * GDN_HEADS


def _mm(a, b):
    return jnp.dot(a.astype(BF16), b.astype(BF16), preferred_element_type=F32)


def _mm_nt(a, b):
    return lax.dot_general(a.astype(BF16), b.astype(BF16), (((1,), (1,)), ((), ())), preferred_element_type=F32)


def _mm_tn(a, b):
    return lax.dot_general(a.astype(BF16), b.astype(BF16), (((0,), (0,)), ((), ())), preferred_element_type=F32)


def _unit_triangular_inverses(lmats, row, col):
    q = lmats[0].shape[0]
    eye = (row == col).astype(F32)

    def same_block(shift):
        return (row >> shift) == (col >> shift)

    blk = same_block(3)
    ds = [jnp.where(blk, m, 0.0) for m in lmats]
    d2s = [_mm(d, d) for d in ds]
    d4s = [_mm(d2, d2) for d2 in d2s]
    xs = [_mm(eye - d, eye + d2) for d, d2 in zip(ds, d2s)]
    xs = [_mm(x, eye + d4) for x, d4 in zip(xs, d4s)]
    shift = 4
    while (1 << shift) <= q:
        level = same_block(shift) & jnp.logical_not(same_block(shift - 1))
        xes = [_mm(x, jnp.where(level, m, 0.0)) for x, m in zip(xs, lmats)]
        xs = [x - _mm(xe, x) for x, xe in zip(xs, xes)]
        shift += 1
    return xs


def _gdn_chunk_operands(q_ref, k_ref, v_ref, n_ref, ap_ref, bias_ref, direction, rows):
    cq = rows.stop - rows.start
    row = lax.broadcasted_iota(jnp.int32, (cq, cq), 0)
    col = lax.broadcasted_iota(jnp.int32, (cq, cq), 1)
    if direction == 0:
        incl, strict = col <= row, col < row
    else:
        incl, strict = col >= row, col > row
    nar = n_ref[rows, :]
    g_all = ap_ref[...] * jax.nn.softplus(nar + bias_ref[...])
    beta_all = jax.nn.sigmoid(nar)
    cum_op = jnp.concatenate([incl.astype(F32), jnp.ones((cq, cq), F32)], axis=0)
    cums = jnp.dot(cum_op, g_all, preferred_element_type=F32, precision=lax.Precision.HIGHEST)
    gc_all, tot_all = cums[:cq], cums[cq:]
    gc_t = gc_all.T
    egc_all = jnp.exp(gc_all)
    erem_all = jnp.exp(tot_all - gc_all)
    etot_all = jnp.exp(tot_all)
    heads = range(GDN_HEADS)
    la = [NARROW_GDN_A + direction * GDN_HEADS + h for h in heads]
    lb = [NARROW_GDN_B + direction * GDN_HEADS + h for h in heads]
    hs = [slice(h * GDN_HEAD_DIM, (h + 1) * GDN_HEAD_DIM) for h in heads]
    q = [q_ref[rows, hs[h]] for h in heads]
    k = [k_ref[rows, hs[h]] for h in heads]
    v = [v_ref[rows, hs[h]] for h in heads]
    beta = [beta_all[:, lb[h]:lb[h] + 1] for h in heads]
    egc = [egc_all[:, la[h]:la[h] + 1] for h in heads]
    decay = [jnp.exp(jnp.where(incl, gc_all[:, la[h]:la[h] + 1] - gc_t[la[h]:la[h] + 1, :], -1e30)) for h in heads]
    kb = [k[h] * beta[h] for h in heads]
    return dict(
        q=q, k=k, decay=decay, strict=[strict] * GDN_HEADS,
        qe=[q[h] * egc[h] for h in heads],
        rhs=[jnp.concatenate([v[h] * beta[h], kb[h] * egc[h]], axis=-1) for h in heads],
        kb=kb,
        ke=[k[h] * erem_all[:, la[h]:la[h] + 1] for h in heads],
        etot=[etot_all[0:1, la[h]:la[h] + 1] for h in heads],
    )


def _gdn_scan_kernel(qf, kf, vf, nf, qb, kb, vb, nb, ap_ref, bias_ref, of_ref, ob_ref, s_ref):
    @pl.when(pl.program_id(1) == 0)
    def _():
        s_ref[...] = jnp.zeros_like(s_ref)

    cq = GDN_CHUNK
    sub = qf.shape[0] // cq
    row = lax.broadcasted_iota(jnp.int32, (cq, cq), 0)
    col = lax.broadcasted_iota(jnp.int32, (cq, cq), 1)
    n = range(2 * GDN_HEADS)
    rows_f = [slice(j * cq, (j + 1) * cq) for j in range(sub)]
    rows_b = rows_f[::-1]
    chunks = []
    for j in range(sub):
        fwd = _gdn_chunk_operands(qf, kf, vf, nf, ap_ref, bias_ref, 0, rows_f[j])
        bwd = _gdn_chunk_operands(qb, kb, vb, nb, ap_ref, bias_ref, 1, rows_b[j])
        chunks.append({name: fwd[name] + bwd[name] for name in fwd})
    kk = [[_mm_nt(c['kb'][i], c['k'][i]) for i in n] for c in chunks]
    qk = [[_mm_nt(c['q'][i], c['k'][i]) for i in n] for c in chunks]
    lmats = [jnp.where(c['strict'][i], kk[j][i] * c['decay'][i], 0.0) for j, c in enumerate(chunks) for i in n]
    t_inv = _unit_triangular_inverses(lmats, row, col)
    s = [s_ref[i] for i in n]
    for j, c in enumerate(chunks):
        uw = [_mm(t_inv[j * len(n) + i], c['rhs'][i]) for i in n]
        ws = [_mm(uw[i][:, GDN_HEAD_DIM:], s[i]) for i in n]
        qs = [_mm(c['qe'][i], s[i]) for i in n]
        v_new = [uw[i][:, :GDN_HEAD_DIM] - ws[i] for i in n]
        outs = [qs[i] + _mm(qk[j][i] * c['decay'][i], v_new[i]) for i in n]
        upd = [_mm_tn(c['ke'][i], v_new[i]) for i in n]
        of_ref[rows_f[j], :] = jnp.concatenate(outs[:GDN_HEADS], axis=-1)
        ob_ref[rows_b[j], :] = jnp.concatenate(outs[GDN_HEADS:], axis=-1)
        s = [s[i] * c['etot'][i] + upd[i] for i in n]
    for i in n:
        s_ref[i] = s[i]


GDN_STEP_CHUNKS = 2


def gdn_scan(qkv, u_narrow, a_par, bias_par, group, carried, *, cq=GDN_STEP_CHUNKS * GDN_CHUNK):
    t = qkv.shape[0]
    w = GDN_WIDTH
    row0, batch, seq = group
    nc = seq // cq
    base = row0 // cq

    def fwd(col):
        return lambda b, c: (base + b * nc + c, col)

    def bwd(col):
        return lambda b, c: (base + b * nc + nc - 1 - c, col)

    wide = lambda idx: pl.BlockSpec((cq, w), idx)
    narrow = lambda idx: pl.BlockSpec((cq, LANES), idx)
    par = pl.BlockSpec((1, LANES), lambda b, c: (0, 0))
    return _group_call(
        _gdn_scan_kernel, "gdn_scan", (batch, nc),
        [wide(fwd(0)), wide(fwd(1)), wide(fwd(2)), narrow(fwd(0)),
         wide(bwd(0)), wide(bwd(1)), wide(bwd(2)), narrow(bwd(0)), par, par],
        (qkv, qkv, qkv, u_narrow, qkv, qkv, qkv, u_narrow, a_par, bias_par),
        [wide(fwd(0)), wide(bwd(0))],
        [jax.ShapeDtypeStruct((t, w), F32), jax.ShapeDtypeStruct((t, w), F32)],
        [pltpu.VMEM((2 * GDN_HEADS, GDN_HEAD_DIM, GDN_HEAD_DIM), F32)], carried)


def _gdn_gate_kernel(of_ref, ob_ref, z_ref, w_ref, o_ref):
    for h in range(GDN_HEADS):
        hs = slice(h * GDN_HEAD_DIM, (h + 1) * GDN_HEAD_DIM)
        o = _rms_normalize(of_ref[:, hs] + ob_ref[:, hs], w_ref[...])
        z = z_ref[:, hs]
        o_ref[:, hs] = o * (z * jax.nn.sigmoid(z))


def gdn_gate(o_f, o_b, u_wide, norm_w, *, tm=512):
    t, w = o_f.shape
    zblk = U_GDN_Z // GDN_WIDTH
    row = lambda i: (i, 0)
    return pl.pallas_call(
        _gdn_gate_kernel,
        grid=(t // tm,),
        in_specs=[pl.BlockSpec((tm, w), row), pl.BlockSpec((tm, w), row), pl.BlockSpec((tm, w), lambda i: (i, zblk)),
                  pl.BlockSpec((1, GDN_HEAD_DIM), lambda i: (0, 0))],
        out_specs=pl.BlockSpec((tm, w), row),
        out_shape=jax.ShapeDtypeStruct((t, w), F32),
        compiler_params=_params(("parallel",)),
        name="gdn_gate",
    )(o_f, o_b, u_wide, norm_w.reshape(1, -1))


def gdn_mixer(u_wide, u_narrow, groups, conv_w, dt_bias, a_log, norm_w):
    ones, zeros = jnp.ones((GDN_WIDTH,), F32), jnp.zeros((GDN_WIDTH,), F32)
    l2 = jnp.stack([jnp.concatenate([ones, ones, zeros]),
                    jnp.concatenate([ones * GDN_HEAD_DIM ** -0.5, ones, zeros])])
    qkv = conv_act(u_wide, U_GDN_QKV, GDN_CONV_DIM, conv_w, jnp.zeros((GDN_CONV_DIM,), F32), groups,
                   l2=l2, head_dim=GDN_HEAD_DIM)
    lanes = slice(NARROW_GDN_A, NARROW_GDN_B)
    a_par = jnp.zeros((1, LANES), F32).at[0, lanes].set(-jnp.exp(a_log).reshape(-1))
    bias_par = jnp.zeros((1, LANES), F32).at[0, lanes].set(dt_bias.reshape(-1))
    outs = None
    for group in groups:
        outs = gdn_scan(qkv, u_narrow, a_par, bias_par, group, outs)
    return gdn_gate(outs[0], outs[1], u_wide, norm_w)


SSD_BC = SSD_GROUPS * SSD_STATE
SSD_REP = SSD_HEADS // SSD_GROUPS


def _ssd_chunk_operands(xbc_ref, n_ref, ap_ref, bias_ref, direction):
    cq = xbc_ref.shape[0]
    row = lax.broadcasted_iota(jnp.int32, (cq, cq), 0)
    col = lax.broadcasted_iota(jnp.int32, (cq, cq), 1)
    incl = col <= row if direction == 0 else col >= row
    dt_all = jax.nn.softplus(n_ref[...] + bias_ref[...])
    cum_op = jnp.concatenate([incl.astype(F32), jnp.ones((cq, cq), F32)], axis=0)
    cums = jnp.dot(cum_op, dt_all * ap_ref[...], preferred_element_type=F32, precision=lax.Precision.HIGHEST)
    ac_all, tot_all = cums[:cq], cums[cq:]
    ac_t = ac_all.T
    eac_all = jnp.exp(ac_all)
    erem_all = jnp.exp(tot_all - ac_all)
    etot_all = jnp.exp(tot_all)
    groups = range(SSD_GROUPS)
    bm = [xbc_ref[:, SSD_WIDTH + g * SSD_STATE:SSD_WIDTH + (g + 1) * SSD_STATE] for g in groups]
    cm = [xbc_ref[:, SSD_WIDTH + SSD_BC + g * SSD_STATE:SSD_WIDTH + SSD_BC + (g + 1) * SSD_STATE] for g in groups]
    cb = [_mm_nt(cm[g], bm[g]) for g in groups]
    heads = range(SSD_HEADS)
    ln = [direction * SSD_HEADS + h for h in heads]
    seg = [jnp.exp(jnp.where(incl, ac_all[:, ln[h]:ln[h] + 1] - ac_t[ln[h]:ln[h] + 1, :], -1e30)) for h in heads]
    return dict(
        scores=[cb[h // SSD_REP] * seg[h] for h in heads],
        xdt=[xbc_ref[:, h * SSD_HEAD_DIM:(h + 1) * SSD_HEAD_DIM] * dt_all[:, ln[h]:ln[h] + 1] for h in heads],
        b_end=[bm[h // SSD_REP] * erem_all[:, ln[h]:ln[h] + 1] for h in heads],
        c_in=[cm[h // SSD_REP] * eac_all[:, ln[h]:ln[h] + 1] for h in heads],
        etot=[etot_all[0:1, ln[h]:ln[h] + 1] for h in heads],
    )


def _ssd_scan_kernel(xf, nf, xb, nb, ap_ref, bias_ref, of_ref, ob_ref, s_ref):
    @pl.when(pl.program_id(1) == 0)
    def _():
        s_ref[...] = jnp.zeros_like(s_ref)

    fwd = _ssd_chunk_operands(xf, nf, ap_ref, bias_ref, 0)
    bwd = _ssd_chunk_operands(xb, nb, ap_ref, bias_ref, 1)
    c = {name: fwd[name] + bwd[name] for name in fwd}
    n = range(2 * SSD_HEADS)
    s = [s_ref[i] for i in n]
    y = [_mm(c['scores'][i], c['xdt'][i]) + _mm(c['c_in'][i], s[i]) for i in n]
    upd = [_mm_tn(c['b_end'][i], c['xdt'][i]) for i in n]
    of_ref[...] = jnp.concatenate(y[:SSD_HEADS], axis=-1)
    ob_ref[...] = jnp.concatenate(y[SSD_HEADS:], axis=-1)
    for i in n:
        s_ref[i] = s[i] * c['etot'][i] + upd[i]


def ssd_scan(xbc, u_narrow, a_par, bias_par, group, carried, *, cq=SSD_CHUNK):
    t = xbc.shape[0]
    row0, batch, seq = group
    nc = seq // cq
    base = row0 // cq

    def fwd(b, c):
        return (base + b * nc + c, 0)

    def bwd(b, c):
        return (base + b * nc + nc - 1 - c, 0)

    wide = lambda idx: pl.BlockSpec((cq, SSD_CONV_DIM), idx)
    narrow = lambda idx: pl.BlockSpec((cq, LANES), idx)
    out = lambda idx: pl.BlockSpec((cq, SSD_WIDTH), idx)
    par = pl.BlockSpec((1, LANES), lambda b, c: (0, 0))
    return _group_call(
        _ssd_scan_kernel, "ssd_scan", (batch, nc),
        [wide(fwd), narrow(fwd), wide(bwd), narrow(bwd), par, par],
        (xbc, u_narrow, xbc, u_narrow, a_par, bias_par),
        [out(fwd), out(bwd)],
        [jax.ShapeDtypeStruct((t, SSD_WIDTH), F32), jax.ShapeDtypeStruct((t, SSD_WIDTH), F32)],
        [pltpu.VMEM((2 * SSD_HEADS, SSD_STATE, SSD_HEAD_DIM), F32)], carried)


def _ssd_gate_kernel(yf_ref, yb_ref, x_ref, z_ref, d_ref, w_ref, o_ref):
    z = z_ref[...]
    y = (yf_ref[...] + yb_ref[...] + x_ref[...] * d_ref[...]) * (z * jax.nn.sigmoid(z))
    o_ref[...] = _rms_normalize(y, w_ref[...])


def ssd_gate(y_f, y_b, xbc, u_wide, d_wide, norm_w, *, tm=512):
    t, w = y_f.shape
    row = lambda i: (i, 0)
    vec = pl.BlockSpec((1, w), lambda i: (0, 0))
    return pl.pallas_call(
        _ssd_gate_kernel,
        grid=(t // tm,),
        in_specs=[pl.BlockSpec((tm, w), row), pl.BlockSpec((tm, w), row), pl.BlockSpec((tm, w), row),
                  pl.BlockSpec((tm, w), lambda i: (i, U_SSD_Z // SSD_WIDTH)), vec, vec],
        out_specs=pl.BlockSpec((tm, w), row),
        out_shape=jax.ShapeDtypeStruct((t, w), F32),
        compiler_params=_params(("parallel",)),
        name="ssd_gate",
    )(y_f, y_b, xbc, u_wide, d_wide, norm_w.reshape(1, -1))


def ssd_mixer(u_wide, u_narrow, groups, conv_w, conv_b, dt_bias, a_log, d_skip, norm_w):
    xbc = conv_act(u_wide, U_SSD_XBC, SSD_CONV_DIM, conv_w, conv_b, groups)
    lanes = slice(0, 2 * SSD_HEADS)
    a_par = jnp.zeros((1, LANES), F32).at[0, lanes].set(-jnp.exp(a_log).reshape(-1))
    bias_par = jnp.zeros((1, LANES), F32).at[0, lanes].set(dt_bias.reshape(-1))
    outs = None
    for group in groups:
        outs = ssd_scan(xbc, u_narrow, a_par, bias_par, group, outs)
    d_wide = jnp.repeat(d_skip, SSD_HEAD_DIM).reshape(1, SSD_WIDTH)
    return ssd_gate(outs[0], outs[1], xbc, u_wide, d_wide, norm_w)


FNET_N2 = 128
FNET_PAIR = 2 * FNET_HEAD_DIM
FNET_SLAB = 16


def _dft_cos_sin(n, scale):
    idx = jnp.arange(n, dtype=jnp.int32)
    ang = (2.0 * math.pi / n) * ((idx[:, None] * idx[None, :]) % n).astype(F32)
    return jnp.cos(ang) * scale, jnp.sin(ang) * scale


def _fnet_channel_kernel(u_ref, m_ref, y_ref):
    for h in range(FNET_HEADS):
        uh = u_ref[:, h * FNET_HEAD_DIM:(h + 1) * FNET_HEAD_DIM].astype(BF16)
        y_ref[h] = jnp.dot(uh, m_ref[...], preferred_element_type=F32).astype(BF16)


def _re_im_products(c_ref, s_ref, x):
    p = jnp.dot(c_ref[...], x, preferred_element_type=F32)
    q = jnp.dot(s_ref[...], x, preferred_element_type=F32)
    res = []
    for j in range(x.shape[1] // FNET_PAIR):
        re = slice(j * FNET_PAIR, j * FNET_PAIR + FNET_HEAD_DIM)
        im = slice(j * FNET_PAIR + FNET_HEAD_DIM, (j + 1) * FNET_PAIR)
        res.append((p[:, re] + q[:, im], p[:, im] - q[:, re]))
    return res


def _fnet_stage1_kernel(x_ref, c_ref, s_ref, tc_ref, ts_ref, o_ref):
    for j, (ar, ai) in enumerate(_re_im_products(c_ref, s_ref, x_ref[0])):
        lanes = slice(j * FNET_HEAD_DIM, (j + 1) * FNET_HEAD_DIM)
        tc, ts = tc_ref[:, lanes], ts_ref[:, lanes]
        o_ref[0, j] = jnp.concatenate([ar * tc + ai * ts, ai * tc - ar * ts], axis=-1).astype(BF16)


def _fnet_stage2_kernel(x_ref, c_ref, s_ref, o_ref):
    o_ref[0] = jnp.concatenate([re for re, _ in _re_im_products(c_ref, s_ref, x_ref[0])], axis=-1)


def _fnet_out_kernel(f_ref, w_ref, b_ref, o_ref):
    acc = b_ref[...]
    for h in range(FNET_HEADS):
        acc = acc + jnp.dot(f_ref[h].astype(BF16), w_ref[h * FNET_HEAD_DIM:(h + 1) * FNET_HEAD_DIM, :],
                            preferred_element_type=F32)
    o_ref[...] = acc


def _fnet_sequence_dft(y, batch, seq, *, slab=FNET_SLAB):
    t = y.shape[1]
    n1, n2 = seq // FNET_N2, FNET_N2
    hb = FNET_HEADS * batch
    c1, s1 = (m.astype(BF16) for m in _dft_cos_sin(n1, n1 ** -0.5))
    k1 = jnp.arange(n1, dtype=jnp.int32)[:, None]
    m2 = jnp.arange(n2, dtype=jnp.int32)[None, :]
    ang = (2.0 * math.pi / seq) * ((k1 * m2) % seq).astype(F32)
    tw_c = jnp.repeat(jnp.cos(ang), FNET_HEAD_DIM, axis=1)
    tw_s = jnp.repeat(jnp.sin(ang), FNET_HEAD_DIM, axis=1)
    mat1 = pl.BlockSpec((n1, n1), lambda g, j: (0, 0))
    a = pl.pallas_call(
        _fnet_stage1_kernel,
        grid=(hb, n2 // slab),
        in_specs=[pl.BlockSpec((1, n1, slab * FNET_PAIR), lambda g, j: (g, 0, j)), mat1, mat1,
                  pl.BlockSpec((n1, slab * FNET_HEAD_DIM), lambda g, j: (0, j)),
                  pl.BlockSpec((n1, slab * FNET_HEAD_DIM), lambda g, j: (0, j))],
        out_specs=pl.BlockSpec((1, slab, n1, FNET_PAIR), lambda g, j: (g, j, 0, 0)),
        out_shape=jax.ShapeDtypeStruct((hb, n2, n1, FNET_PAIR), BF16),
        compiler_params=_params(("parallel", "parallel")),
        name="fnet_stage1",
    )(y.reshape(hb, n1, n2 * FNET_PAIR), c1, s1, tw_c, tw_s)

    c2, s2 = (m.astype(BF16) for m in _dft_cos_sin(n2, n2 ** -0.5))
    kslab = min(slab, n1)
    mat2 = pl.BlockSpec((n2, n2), lambda g, j: (0, 0))
    f = pl.pallas_call(
        _fnet_stage2_kernel,
        grid=(hb, n1 // kslab),
        in_specs=[pl.BlockSpec((1, n2, kslab * FNET_PAIR), lambda g, j: (g, 0, j)), mat2, mat2],
        out_specs=pl.BlockSpec((1, n2, kslab * FNET_HEAD_DIM), lambda g, j: (g, 0, j)),
        out_shape=jax.ShapeDtypeStruct((hb, n2, n1 * FNET_HEAD_DIM), F32),
        compiler_params=_params(("parallel", "parallel")),
        name="fnet_stage2",
    )(a.reshape(hb, n2, n1 * FNET_PAIR), c2, s2)
    return f.reshape(FNET_HEADS, t, FNET_HEAD_DIM)


def fourier_mixer(u_wide, groups, w, b, *, tm=512):
    t = u_wide.shape[0]
    cc, sc = _dft_cos_sin(FNET_HEAD_DIM, FNET_HEAD_DIM ** -0.5)
    y = pl.pallas_call(
        _fnet_channel_kernel,
        grid=(t // tm,),
        in_specs=[pl.BlockSpec((tm, FNET_WIDTH), lambda i: (i, U_FNET // FNET_WIDTH)),
                  pl.BlockSpec((FNET_HEAD_DIM, FNET_PAIR), lambda i: (0, 0))],
        out_specs=pl.BlockSpec((FNET_HEADS, tm, FNET_PAIR), lambda i: (0, i, 0)),
        out_shape=jax.ShapeDtypeStruct((FNET_HEADS, t, FNET_PAIR), BF16),
        compiler_params=_params(("parallel",)),
        name="fnet_channel",
    )(u_wide, jnp.concatenate([cc, -sc], axis=-1).astype(BF16))
    f = jnp.concatenate([_fnet_sequence_dft(y[:, row0:row0 + batch * seq], batch, seq) for row0, batch, seq in groups],
                        axis=1)
    return pl.pallas_call(
        _fnet_out_kernel,
        grid=(t // tm,),
        in_specs=[pl.BlockSpec((FNET_HEADS, tm, FNET_HEAD_DIM), lambda i: (0, i, 0)),
                  pl.BlockSpec((FNET_WIDTH, FNET_WIDTH), lambda i: (0, 0)),
                  pl.BlockSpec((1, FNET_WIDTH), lambda i: (0, 0))],
        out_specs=pl.BlockSpec((tm, FNET_WIDTH), lambda i: (i, 0)),
        out_shape=jax.ShapeDtypeStruct((t, FNET_WIDTH), F32),
        compiler_params=_params(("parallel",)),
        name="fnet_out",
    )(f, w.astype(BF16), b.reshape(1, -1))


S5_TILE = 256
S5_NSTATE = S5_GROUPS * S5_STATE
S5_COLS = 512
S5_BLOCKS = S5_WIDTH // LANES


def _s5_scan_kernel(u_ref, pi_ref, pit_ref, b_ref, c_ref, lre_ref, lim_ref, o_ref,
                    sre, sim, pre, pim, car_re, car_im, e_re, e_im, cin_re, cin_im):
    tl = u_ref.shape[0]
    steps = tl // 8
    first_tile = pl.program_id(2) == 0

    def col(j):
        return slice(j * S5_COLS, (j + 1) * S5_COLS)

    def rows(t):
        return pl.ds(pl.multiple_of(t * 8, 8), 8)

    def bcast(ref, j):
        return jnp.broadcast_to(ref[0, :, col(j)], (8, S5_COLS))

    @pl.when(first_tile)
    def _():
        car_re[...] = jnp.zeros_like(car_re)
        car_im[...] = jnp.zeros_like(car_im)
        for j in range(S5_BLOCKS):
            lr, li = bcast(lre_ref, j), bcast(lim_ref, j)

            def power_step(t, p, j=j, lr=lr, li=li):
                pr, pi_ = p
                pre[rows(t), col(j)] = pr
                pim[rows(t), col(j)] = pi_
                return pr * lr - pi_ * li, pr * li + pi_ * lr

            lax.fori_loop(0, steps, power_step, (lr, li))

    up = jnp.dot(pi_ref[0], u_ref[...].astype(BF16), preferred_element_type=F32).astype(BF16)
    for j in range(S5_BLOCKS):
        uj = up[:, j * LANES:(j + 1) * LANES]
        sre[:, col(j)] = jnp.dot(uj, b_ref[0, j, 0], preferred_element_type=F32)
        sim[:, col(j)] = jnp.dot(uj, b_ref[0, j, 1], preferred_element_type=F32)

    for j in range(S5_BLOCKS):
        lr, li = bcast(lre_ref, j), bcast(lim_ref, j)

        def scan_step(t, s, j=j, lr=lr, li=li):
            sr, si = s
            nr = lr * sr - li * si + sre[rows(t), col(j)]
            ni = lr * si + li * sr + sim[rows(t), col(j)]
            sre[rows(t), col(j)] = nr
            sim[rows(t), col(j)] = ni
            return nr, ni

        zero = jnp.zeros((8, S5_COLS), F32)
        er, ei = lax.fori_loop(0, steps, scan_step, (zero, zero), unroll=4)
        e_re[:, col(j)] = er
        e_im[:, col(j)] = ei

    last = pl.ds((steps - 1) * 8, 1)
    lsr, lsi = pre[last, :], pim[last, :]
    cr, ci = car_re[...], car_im[...]
    for r in range(8):
        cin_re[r:r + 1, :] = cr
        cin_im[r:r + 1, :] = ci
        cr, ci = (e_re[r:r + 1, :] + lsr * cr - lsi * ci, e_im[r:r + 1, :] + lsr * ci + lsi * cr)
    car_re[...] = cr
    car_im[...] = ci

    for j in range(S5_BLOCKS):
        cr, ci = cin_re[:, col(j)], cin_im[:, col(j)]

        def fix_step(t, carry, j=j, cr=cr, ci=ci):
            pr, pi_ = pre[rows(t), col(j)], pim[rows(t), col(j)]
            sre[rows(t), col(j)] = sre[rows(t), col(j)] + pr * cr - pi_ * ci
            sim[rows(t), col(j)] = sim[rows(t), col(j)] + pr * ci + pi_ * cr
            return carry

        lax.fori_loop(0, steps, fix_step, 0, unroll=4)

    ys = []
    for j in range(S5_BLOCKS):
        ys.append(jnp.dot(sre[:, col(j)].astype(BF16), c_ref[0, j, 0], preferred_element_type=F32)
                  + jnp.dot(sim[:, col(j)].astype(BF16), c_ref[0, j, 1], preferred_element_type=F32))
    y = jnp.concatenate(ys, axis=-1)
    y_hi = y.astype(BF16)
    y_lo = (y - y_hi.astype(F32)).astype(BF16)
    pit = pit_ref[0]
    o_ref[0] = jnp.dot(pit, y_hi, preferred_element_type=F32) + jnp.dot(pit, y_lo, preferred_element_type=F32)


def _s5_operators(lam_re, lam_im, log_step, b_re, b_im, c_re, c_im, tl):
    lam = lax.complex(lam_re, lam_im)
    delta = jnp.exp(log_step)[..., None]
    lam_bar = jnp.exp(lam * delta)
    b_scale = (lam_bar - 1.0) / lam
    gpb = S5_GROUPS // S5_BLOCKS
    same_group = (jnp.arange(LANES)[:, None] // S5_GROUP_CH) == (jnp.arange(S5_COLS)[None, :] // S5_STATE)
    bd = lax.complex(b_re, b_im)[None] * b_scale[..., None]
    bd = jnp.swapaxes(bd.reshape(2, S5_BLOCKS, gpb, S5_STATE, S5_GROUP_CH), 3, 4).reshape(2, S5_BLOCKS, LANES, S5_STATE)
    bd = jnp.tile(bd, (1, 1, 1, gpb))
    bmat = jnp.stack([jnp.where(same_group, part, 0.0) for part in (bd.real, bd.imag)], axis=2).astype(BF16)
    cd = lax.complex(c_re, c_im).reshape(2, S5_BLOCKS, gpb, S5_GROUP_CH, S5_STATE)
    cd = jnp.transpose(cd, (0, 1, 4, 2, 3)).reshape(2, S5_BLOCKS, S5_STATE, LANES)
    cd = jnp.tile(cd, (1, 1, gpb, 1))
    cmat = jnp.stack([jnp.where(same_group.T, part, 0.0) for part in (cd.real, -cd.imag)], axis=2).astype(BF16)
    lre = lam_bar.real.reshape(2, 1, S5_NSTATE)
    lim = lam_bar.imag.reshape(2, 1, S5_NSTATE)
    steps = tl // 8
    rt = jnp.arange(tl)
    src = (rt % 8) * steps + rt // 8
    fwd = (src[:, None] == rt[None, :])
    bwd = ((tl - 1 - src)[:, None] == rt[None, :])
    pi = jnp.stack([fwd, bwd]).astype(BF16)
    return pi, jnp.swapaxes(pi, 1, 2), bmat, cmat, lre, lim


def s5_scan(u_wide, ops, group, carried, *, tl=S5_TILE):
    pi, pit, bmat, cmat, lre, lim = ops
    t = u_wide.shape[0]
    row0, batch, seq = group
    nt = seq // tl
    base = row0 // tl
    ublk = U_S5 // S5_WIDTH

    def tile_idx(b, d, i):
        return base + b * nt + jnp.where(d == 0, i, nt - 1 - i)

    state = pltpu.VMEM((tl, S5_NSTATE), F32)
    row = pltpu.VMEM((1, S5_NSTATE), F32)
    seg = pltpu.VMEM((8, S5_NSTATE), F32)
    return _group_call(
        _s5_scan_kernel, "s5_scan", (batch, 2, nt),
        [
            pl.BlockSpec((tl, S5_WIDTH), lambda b, d, i: (tile_idx(b, d, i), ublk)),
            pl.BlockSpec((1, tl, tl), lambda b, d, i: (d, 0, 0)),
            pl.BlockSpec((1, tl, tl), lambda b, d, i: (d, 0, 0)),
            pl.BlockSpec((1, S5_BLOCKS, 2, LANES, S5_COLS), lambda b, d, i: (d, 0, 0, 0, 0)),
            pl.BlockSpec((1, S5_BLOCKS, 2, S5_COLS, LANES), lambda b, d, i: (d, 0, 0, 0, 0)),
            pl.BlockSpec((1, 1, S5_NSTATE), lambda b, d, i: (d, 0, 0)),
            pl.BlockSpec((1, 1, S5_NSTATE), lambda b, d, i: (d, 0, 0)),
        ],
        (u_wide, pi, pit, bmat, cmat, lre, lim),
        [pl.BlockSpec((1, tl, S5_WIDTH), lambda b, d, i: (d, tile_idx(b, d, i), 0))],
        [jax.ShapeDtypeStruct((2, t, S5_WIDTH), F32)],
        [state, state, state, state, row, row, seg, seg, seg, seg], carried)[0]


def _s5_gate_kernel(y_ref, u_ref, d_ref, w_ref, b_ref, o_ref):
    y = y_ref[0] + y_ref[1] + u_ref[...] * d_ref[...]
    h = jax.nn.gelu(y)
    gate = jnp.dot(h.astype(BF16), w_ref[...], preferred_element_type=F32) + b_ref[...]
    o_ref[...] = h * jax.nn.sigmoid(gate)


def s5_gate(y2, u_wide, d_skip, glu_w, glu_b, *, tm=512):
    t = u_wide.shape[0]
    ublk = U_S5 // S5_WIDTH
    return pl.pallas_call(
        _s5_gate_kernel,
        grid=(t // tm,),
        in_specs=[
            pl.BlockSpec((2, tm, S5_WIDTH), lambda i: (0, i, 0)),
            pl.BlockSpec((tm, S5_WIDTH), lambda i: (i, ublk)),
            pl.BlockSpec((1, S5_WIDTH), lambda i: (0, 0)),
            pl.BlockSpec((S5_WIDTH, S5_WIDTH), lambda i: (0, 0)),
            pl.BlockSpec((1, S5_WIDTH), lambda i: (0, 0)),
        ],
        out_specs=pl.BlockSpec((tm, S5_WIDTH), lambda i: (i, 0)),
        out_shape=jax.ShapeDtypeStruct((t, S5_WIDTH), F32),
        compiler_params=_params(("parallel",)),
        name="s5_gate",
    )(y2, u_wide, d_skip.reshape(1, -1), glu_w.astype(BF16), glu_b.reshape(1, -1))


def s5_mixer(u_wide, groups, ops, d_skip, glu_w, glu_b):
    y2 = None
    for group in groups:
        y2 = s5_scan(u_wide, ops, group, None if y2 is None else (y2,))
    return s5_gate(y2, u_wide, d_skip, glu_w, glu_b)


def mixers(u_wide, u_narrow, groups, p, i):
    y_ssd = ssd_mixer(u_wide, u_narrow, groups, p['ssd_conv_w'][i], p['ssd_conv_b'][i], p['ssd_dt_bias'][i],
                      p['ssd_a_log'][i], p['ssd_d'][i], p['ssd_norm'][i])
    y_gdn = gdn_mixer(u_wide, u_narrow, groups, p['gdn_conv_w'][i], p['gdn_dt_bias'][i], p['gdn_a_log'][i],
                      p['gdn_norm'][i])
    y_fnet = fourier_mixer(u_wide, groups, p['fnet_w'][i], p['fnet_b'][i])
    s5_ops = _s5_operators(p['s5_lambda_re'][i], p['s5_lambda_im'][i], p['s5_log_step'][i], p['s5_b_re'][i],
                           p['s5_b_im'][i], p['s5_c_re'][i], p['s5_c_im'][i], S5_TILE)
    y_s5 = s5_mixer(u_wide, groups, s5_ops, p['s5_d'][i], p['s5_glu_w'][i], p['s5_glu_b'][i])
    return y_ssd, y_gdn, y_fnet, y_s5


def _split_w_in(w_in):
    o1 = SSD_IN
    o2 = o1 + GDN_IN
    ssd_z = w_in[..., :SSD_WIDTH]
    ssd_xbc = w_in[..., SSD_WIDTH:SSD_WIDTH + SSD_CONV_DIM]
    ssd_dt = w_in[..., SSD_WIDTH + SSD_CONV_DIM:o1]
    gdn_qkv = w_in[..., o1:o1 + GDN_CONV_DIM]
    gdn_z = w_in[..., o1 + GDN_CONV_DIM:o1 + 4 * GDN_WIDTH]
    gdn_ab = w_in[..., o1 + 4 * GDN_WIDTH:o2]
    rest = w_in[..., o2:]
    wide = jnp.concatenate([ssd_z, ssd_xbc, gdn_qkv, gdn_z, rest], axis=-1).astype(BF16)
    pad = jnp.zeros(w_in.shape[:-1] + (LANES - NARROW,), w_in.dtype)
    narrow = jnp.concatenate([ssd_dt, gdn_ab, pad], axis=-1).astype(BF16)
    return wide, narrow


def kernel(x_prompt, x_sample, norm_mix, w_in, ssd_conv_w, ssd_conv_b, ssd_dt_bias, ssd_a_log, ssd_d, ssd_norm,
           gdn_conv_w, gdn_dt_bias, gdn_a_log, gdn_norm, fnet_w, fnet_b, s5_lambda_re, s5_lambda_im, s5_log_step,
           s5_b_re, s5_b_im, s5_c_re, s5_c_im, s5_d, s5_glu_w, s5_glu_b, w_out, norm_ffn, ffn_w_gate, ffn_w_up,
           ffn_w_down, router_w, moe_w_gate, moe_w_up, moe_w_down, norm_final):
    p = dict(ssd_conv_w=ssd_conv_w, ssd_conv_b=ssd_conv_b, ssd_dt_bias=ssd_dt_bias, ssd_a_log=ssd_a_log, ssd_d=ssd_d,
             ssd_norm=ssd_norm, gdn_conv_w=gdn_conv_w, gdn_dt_bias=gdn_dt_bias, gdn_a_log=gdn_a_log,
             gdn_norm=gdn_norm, fnet_w=fnet_w, fnet_b=fnet_b, s5_lambda_re=s5_lambda_re, s5_lambda_im=s5_lambda_im,
             s5_log_step=s5_log_step, s5_b_re=s5_b_re, s5_b_im=s5_b_im, s5_c_re=s5_c_re, s5_c_im=s5_c_im, s5_d=s5_d,
             s5_glu_w=s5_glu_w, s5_glu_b=s5_glu_b)
    bp, lp, d = x_prompt.shape
    bs, ls, _ = x_sample.shape
    tp = bp * lp
    x = jnp.concatenate([x_prompt.reshape(tp, d), x_sample.reshape(bs * ls, d)], axis=0)
    groups = ((0, bp, lp), (tp, bs, ls))
    w_wide, w_narrow = _split_w_in(w_in)
    w_out_b = w_out.astype(BF16)
    ffn_w_gate, ffn_w_up, ffn_w_down = (w.astype(BF16) for w in (ffn_w_gate, ffn_w_up, ffn_w_down))
    depth = w_in.shape[0]
    for i in range(depth):
        u_wide, u_narrow = norm_inproj(x, norm_mix[i], w_wide[i], w_narrow[i])
        x = outproj_residual(mixers(u_wide, u_narrow, groups, p, i), w_out_b[i], x)
        j = i // 2
        if i % 2 == 0:
            x = dense_ffn(x, norm_ffn[i], ffn_w_gate, ffn_w_up, ffn_w_down, j)
        else:
            x = moe_block(x, norm_ffn[i], router_w[j], moe_w_gate, moe_w_up, moe_w_down, j)
    y = final_norm(x, norm_final)
    return y[:tp].reshape(bp, lp, d), y[tp:].reshape(bs, ls, d)
```

```python
import functools
import math

import jax
import jax.numpy as jnp
from jax import lax
from jax.experimental import pallas as pl
from jax.experimental.pallas import tpu as pltpu

F32 = jnp.float32
BF16 = jnp.bfloat16

D_MODEL = 2048
DEPTH = 4
GROUP_WIDTH = 512

SSD_WIDTH = GROUP_WIDTH
SSD_HEAD_DIM = 64
SSD_HEADS = SSD_WIDTH // SSD_HEAD_DIM
SSD_GROUPS = 2
SSD_STATE = 128
SSD_CHUNK = 128
SSD_CONV_DIM = SSD_WIDTH + 2 * SSD_GROUPS * SSD_STATE

GDN_WIDTH = GROUP_WIDTH
GDN_HEAD_DIM = 128
GDN_HEADS = GDN_WIDTH // GDN_HEAD_DIM
GDN_CHUNK = 64
GDN_CONV_DIM = 3 * GDN_WIDTH

CONV_K = 5

FNET_WIDTH = GROUP_WIDTH
FNET_HEADS = 4
FNET_HEAD_DIM = FNET_WIDTH // FNET_HEADS

S5_WIDTH = GROUP_WIDTH
S5_GROUP_CH = 16
S5_GROUPS = S5_WIDTH // S5_GROUP_CH
S5_STATE = 64

SSD_IN = 2 * SSD_WIDTH + 2 * SSD_GROUPS * SSD_STATE + 2 * SSD_HEADS
GDN_IN = 4 * GDN_WIDTH + 4 * GDN_HEADS
IN_COLS = SSD_IN + GDN_IN + FNET_WIDTH + S5_WIDTH

N_EXPERTS = 8
TOP_K = 2
EPS = 1e-6

LANES = 128
VMEM_LIMIT_BYTES = 56 * 1024 * 1024

U_SSD_Z = 0
U_SSD_XBC = U_SSD_Z + SSD_WIDTH
U_GDN_QKV = U_SSD_XBC + SSD_CONV_DIM
U_GDN_Z = U_GDN_QKV + GDN_CONV_DIM
U_FNET = U_GDN_Z + GDN_WIDTH
U_S5 = U_FNET + FNET_WIDTH
U_WIDE = U_S5 + S5_WIDTH
NARROW = 2 * SSD_HEADS + 4 * GDN_HEADS

ROW_TILE = 1024
FF_TILE = 256


def _params(semantics):
    return pltpu.CompilerParams(dimension_semantics=semantics, vmem_limit_bytes=VMEM_LIMIT_BYTES)


def _drop_refs(kernel, start, count, *refs):
    return kernel(*refs[:start], *refs[start + count:])


def _group_call(kernel, name, grid, in_specs, args, out_specs, out_shape, scratch_shapes, carried):
    aliases = {}
    if carried is not None:
        n_in = len(args)
        kernel = functools.partial(_drop_refs, kernel, n_in, len(carried))
        in_specs = list(in_specs) + [pl.BlockSpec(memory_space=pl.ANY)] * len(carried)
        args = tuple(args) + tuple(carried)
        aliases = {n_in + k: k for k in range(len(carried))}
    return pl.pallas_call(
        kernel, grid=grid, in_specs=in_specs, out_specs=out_specs, out_shape=out_shape,
        scratch_shapes=scratch_shapes, input_output_aliases=aliases,
        compiler_params=_params(("arbitrary",) * len(grid)), name=name,
    )(*args)


def _rms_normalize(x, gain):
    ms = jnp.mean(x * x, axis=-1, keepdims=True)
    return x * lax.rsqrt(ms + EPS) * gain


def _norm_inproj_kernel(x_ref, g_ref, w_ref, wn_ref, o_ref, on_ref, h_ref):
    @pl.when(pl.program_id(1) == 0)
    def _():
        h = _rms_normalize(x_ref[...], g_ref[...]).astype(BF16)
        h_ref[...] = h
        on_ref[...] = jnp.dot(h, wn_ref[...], preferred_element_type=F32)

    o_ref[...] = jnp.dot(h_ref[...], w_ref[...], preferred_element_type=F32)


def norm_inproj(x, gain, w_wide, w_narrow, *, tm=ROW_TILE, tn=512):
    t, d = x.shape
    n = w_wide.shape[1]
    return pl.pallas_call(
        _norm_inproj_kernel,
        grid=(t // tm, n // tn),
        in_specs=[
            pl.BlockSpec((tm, d), lambda i, j: (i, 0)),
            pl.BlockSpec((1, d), lambda i, j: (0, 0)),
            pl.BlockSpec((d, tn), lambda i, j: (0, j)),
            pl.BlockSpec((d, LANES), lambda i, j: (0, 0)),
        ],
        out_specs=[
            pl.BlockSpec((tm, tn), lambda i, j: (i, j)),
            pl.BlockSpec((tm, LANES), lambda i, j: (i, 0)),
        ],
        out_shape=[jax.ShapeDtypeStruct((t, n), F32), jax.ShapeDtypeStruct((t, LANES), F32)],
        scratch_shapes=[pltpu.VMEM((tm, d), BF16)],
        compiler_params=_params(("parallel", "arbitrary")),
        name="norm_inproj",
    )(x, gain.reshape(1, d), w_wide, w_narrow)


def _outproj_kernel(*refs):
    *y_refs, w_ref, x_ref, o_ref = refs
    acc = x_ref[...]
    for m, y_ref in enumerate(y_refs):
        k = y_ref.shape[1]
        acc = acc + jnp.dot(y_ref[...].astype(BF16), w_ref[m * k:(m + 1) * k, :], preferred_element_type=F32)
    o_ref[...] = acc


def outproj_residual(ys, w, x, *, tm=512):
    t, d = x.shape
    row = lambda i: (i, 0)
    return pl.pallas_call(
        _outproj_kernel,
        grid=(t // tm,),
        in_specs=[pl.BlockSpec((tm, y.shape[1]), row) for y in ys] + [
            pl.BlockSpec(w.shape, lambda i: (0, 0)),
            pl.BlockSpec((tm, d), row),
        ],
        out_specs=pl.BlockSpec((tm, d), row),
        out_shape=jax.ShapeDtypeStruct((t, d), F32),
        compiler_params=_params(("parallel",)),
        name="outproj_residual",
    )(*ys, w, x)


def _swiglu_step(h, wg_ref, wu_ref, wd_ref, acc_ref):
    g = jnp.dot(h, wg_ref[0].astype(BF16), preferred_element_type=F32)
    u = jnp.dot(h, wu_ref[0].astype(BF16), preferred_element_type=F32)
    a = (g * jax.nn.sigmoid(g) * u).astype(BF16)
    acc_ref[...] += jnp.dot(a, wd_ref[0].astype(BF16), preferred_element_type=F32)


def _dense_ffn_kernel(x_ref, g_ref, wg_ref, wu_ref, wd_ref, o_ref, h_ref, acc_ref):
    f = pl.program_id(1)

    @pl.when(f == 0)
    def _():
        h_ref[...] = _rms_normalize(x_ref[...], g_ref[...]).astype(BF16)
        acc_ref[...] = jnp.zeros_like(acc_ref)

    _swiglu_step(h_ref[...], wg_ref, wu_ref, wd_ref, acc_ref)

    @pl.when(f == pl.num_programs(1) - 1)
    def _():
        o_ref[...] = x_ref[...] + acc_ref[...]


def dense_ffn(x, gain, wg, wu, wd, layer, *, tm=512, tf=512):
    t, d = x.shape
    ff = wg.shape[2]
    return pl.pallas_call(
        _dense_ffn_kernel,
        grid=(t // tm, ff // tf),
        in_specs=[
            pl.BlockSpec((tm, d), lambda i, f: (i, 0)),
            pl.BlockSpec((1, d), lambda i, f: (0, 0)),
            pl.BlockSpec((1, d, tf), lambda i, f: (layer, 0, f)),
            pl.BlockSpec((1, d, tf), lambda i, f: (layer, 0, f)),
            pl.BlockSpec((1, tf, d), lambda i, f: (layer, f, 0)),
        ],
        out_specs=pl.BlockSpec((tm, d), lambda i, f: (i, 0)),
        out_shape=jax.ShapeDtypeStruct((t, d), F32),
        scratch_shapes=[pltpu.VMEM((tm, d), BF16), pltpu.VMEM((tm, d), F32)],
        compiler_params=_params(("parallel", "arbitrary")),
        name="dense_ffn",
    )(x, gain.reshape(1, d), wg, wu, wd)


def _moe_ffn_kernel(te_ref, na_ref, x_ref, wg_ref, wu_ref, wd_ref, o_ref, acc_ref):
    i = pl.program_id(0)
    f = pl.program_id(1)

    @pl.when(i < na_ref[0])
    def _():
        @pl.when(f == 0)
        def _():
            acc_ref[...] = jnp.zeros_like(acc_ref)

        _swiglu_step(x_ref[...], wg_ref, wu_ref, wd_ref, acc_ref)

        @pl.when(f == pl.num_programs(1) - 1)
        def _():
            o_ref[...] = acc_ref[...]


def moe_ffn_grouped(xs, tile_expert, n_active, wg, wu, wd, layer, *, tm=ROW_TILE, tf=FF_TILE):
    p, d = xs.shape
    n_e, ff = wg.shape[1], wg.shape[3]
    n_f = ff // tf
    wg = wg.reshape(-1, d, ff)
    wu = wu.reshape(-1, d, ff)
    wd = wd.reshape(-1, ff, d)

    def row_idx(i, f, te, na):
        return (jnp.minimum(i, na[0] - 1), 0)

    def up_idx(i, f, te, na):
        live = i < na[0]
        return (layer * n_e + te[jnp.minimum(i, na[0] - 1)], 0, jnp.where(live, f, n_f - 1))

    def down_idx(i, f, te, na):
        live = i < na[0]
        return (layer * n_e + te[jnp.minimum(i, na[0] - 1)], jnp.where(live, f, n_f - 1), 0)

    grid_spec = pltpu.PrefetchScalarGridSpec(
        num_scalar_prefetch=2,
        grid=(p // tm, n_f),
        in_specs=[
            pl.BlockSpec((tm, d), row_idx),
            pl.BlockSpec((1, d, tf), up_idx),
            pl.BlockSpec((1, d, tf), up_idx),
            pl.BlockSpec((1, tf, d), down_idx),
        ],
        out_specs=pl.BlockSpec((tm, d), row_idx),
        scratch_shapes=[pltpu.VMEM((tm, d), F32)],
    )
    return pl.pallas_call(
        _moe_ffn_kernel,
        grid_spec=grid_spec,
        out_shape=jax.ShapeDtypeStruct((p, d), F32),
        compiler_params=_params(("arbitrary", "arbitrary")),
        name="moe_ffn",
    )(tile_expert, n_active, xs, wg, wu, wd)


def _norm_router_kernel(x_ref, g_ref, wr_ref, h_ref, l_ref):
    h = _rms_normalize(x_ref[...], g_ref[...])
    h_ref[...] = h.astype(BF16)
    l_ref[...] = jnp.dot(h, wr_ref[...], preferred_element_type=F32, precision=lax.Precision.HIGHEST)


def norm_router(x, gain, router_w_padded, *, tm=512):
    t, d = x.shape
    return pl.pallas_call(
        _norm_router_kernel,
        grid=(t // tm,),
        in_specs=[
            pl.BlockSpec((tm, d), lambda i: (i, 0)),
            pl.BlockSpec((1, d), lambda i: (0, 0)),
            pl.BlockSpec((d, LANES), lambda i: (0, 0)),
        ],
        out_specs=[pl.BlockSpec((tm, d), lambda i: (i, 0)), pl.BlockSpec((tm, LANES), lambda i: (i, 0))],
        out_shape=[jax.ShapeDtypeStruct((t, d), BF16), jax.ShapeDtypeStruct((t, LANES), F32)],
        compiler_params=_params(("parallel",)),
        name="norm_router",
    )(x, gain.reshape(1, d), router_w_padded)


def _final_norm_kernel(x_ref, g_ref, o_ref):
    o_ref[...] = _rms_normalize(x_ref[...], g_ref[...])


def final_norm(x, gain, *, tm=512):
    t, d = x.shape
    return pl.pallas_call(
        _final_norm_kernel,
        grid=(t // tm,),
        in_specs=[pl.BlockSpec((tm, d), lambda i: (i, 0)), pl.BlockSpec((1, d), lambda i: (0, 0))],
        out_specs=pl.BlockSpec((tm, d), lambda i: (i, 0)),
        out_shape=jax.ShapeDtypeStruct((t, d), F32),
        compiler_params=_params(("parallel",)),
        name="final_norm",
    )(x, gain.reshape(1, d))


def moe_block(x, gain, router_w, wg, wu, wd, layer, *, tm=ROW_TILE, tf=FF_TILE):
    t, d = x.shape
    n_e = router_w.shape[1]
    wr = jnp.zeros((d, LANES), F32).at[:, :n_e].set(router_w)
    h, logits = norm_router(x, gain, wr)
    top_logit, top_idx = lax.top_k(logits[:, :n_e], TOP_K)
    top_w = jax.nn.softmax(top_logit, axis=-1)

    n_assign = t * TOP_K
    n_tiles = (n_assign + n_e * (tm - 1)) // tm + 1
    flat_e = top_idx.reshape(-1).astype(jnp.int32)
    order = jnp.argsort(flat_e, stable=True).astype(jnp.int32)
    rank = jnp.argsort(order).astype(jnp.int32)
    counts = jnp.sum((flat_e[:, None] == jnp.arange(n_e, dtype=jnp.int32)[None, :]).astype(jnp.int32), axis=0)
    padded = ((counts + tm - 1) // tm) * tm
    pad_end = jnp.cumsum(padded)
    pad_start = pad_end - padded
    start = jnp.cumsum(counts) - counts
    tile_start = jnp.arange(n_tiles, dtype=jnp.int32) * tm
    tile_expert = jnp.minimum(jnp.searchsorted(pad_end, tile_start, side="right"), n_e - 1).astype(jnp.int32)
    n_active = (pad_end[-1] // tm).astype(jnp.int32).reshape(1)
    row_e = jnp.repeat(tile_expert, tm)
    within = jnp.arange(n_tiles * tm, dtype=jnp.int32) - pad_start[row_e]
    sorted_pos = jnp.clip(start[row_e] + within, 0, n_assign - 1)
    src_token = jnp.where(within < counts[row_e], order[sorted_pos] // TOP_K, 0)
    pos = (pad_start[flat_e] + rank - start[flat_e]).reshape(t, TOP_K)

    xs = jnp.take(h, src_token, axis=0)
    ys = moe_ffn_grouped(xs, tile_expert, n_active, wg, wu, wd, layer, tm=tm, tf=tf)
    return (x + top_w[:, 0:1] * jnp.take(ys, pos[:, 0], axis=0) + top_w[:, 1:2] * jnp.take(ys, pos[:, 1], axis=0))


CONV_HALO = 8
CONV_COLS = 512


def _sequence_edges(tile, tm, groups):
    first = last = tile < 0
    for row0, batch, seq in groups:
        tiles_per_seq = seq // tm
        rel = tile - row0 // tm
        inside = jnp.logical_and(rel >= 0, rel < batch * tiles_per_seq)
        pos = jnp.maximum(rel, 0) % tiles_per_seq
        first = jnp.logical_or(first, jnp.logical_and(inside, pos == 0))
        last = jnp.logical_or(last, jnp.logical_and(inside, pos == tiles_per_seq - 1))
    return first, last


def _conv_act_kernel(prev_ref, cur_ref, next_ref, w_ref, b_ref, l2_ref, o_ref, buf, *, groups, head_dim):
    tm = cur_ref.shape[0]
    first, last = _sequence_edges(pl.program_id(0), tm, groups)
    buf[0:CONV_HALO, :] = jnp.where(first, 0.0, prev_ref[...])
    buf[CONV_HALO:CONV_HALO + tm, :] = cur_ref[...]
    buf[CONV_HALO + tm:2 * CONV_HALO + tm, :] = jnp.where(last, 0.0, next_ref[...])
    acc = jnp.broadcast_to(b_ref[...], (tm, cur_ref.shape[1]))
    for k in range(CONV_K):
        acc = acc + buf[pl.ds(CONV_HALO - CONV_K // 2 + k, tm), :] * w_ref[k:k + 1, :]
    y = acc * jax.nn.sigmoid(acc)
    if head_dim is None:
        o_ref[...] = y
    else:
        for h in range(y.shape[1] // head_dim):
            hs = slice(h * head_dim, (h + 1) * head_dim)
            yh = y[:, hs]
            normed = yh * lax.rsqrt(jnp.sum(yh * yh, axis=-1, keepdims=True) + EPS) * l2_ref[1:2, hs]
            o_ref[:, hs] = jnp.where(l2_ref[0:1, hs] > 0.5, normed, yh)


def conv_act(u_wide, col0, width, conv_w, conv_b, groups, *, l2=None, head_dim=None, tm=1024):
    tm = min([tm] + [seq for _, _, seq in groups])
    t = u_wide.shape[0]
    nrow = t // tm
    nhalo = tm // CONV_HALO
    cb0 = col0 // CONV_COLS
    w = jnp.zeros((CONV_HALO, width), F32).at[:CONV_K].set(conv_w.T)
    if l2 is None:
        l2 = jnp.zeros((2, width), F32)
    l2 = jnp.zeros((CONV_HALO, width), F32).at[:2].set(l2)
    kern = functools.partial(_conv_act_kernel, groups=groups, head_dim=head_dim)
    par = pl.BlockSpec((CONV_HALO, CONV_COLS), lambda i, j: (0, j))
    return pl.pallas_call(
        kern,
        grid=(nrow, width // CONV_COLS),
        in_specs=[
            pl.BlockSpec((CONV_HALO, CONV_COLS), lambda i, j: (jnp.maximum(i * nhalo - 1, 0), cb0 + j)),
            pl.BlockSpec((tm, CONV_COLS), lambda i, j: (i, cb0 + j)),
            pl.BlockSpec((CONV_HALO, CONV_COLS), lambda i, j: (jnp.minimum((i + 1) * nhalo, t // CONV_HALO - 1), cb0 + j)),
            par,
            pl.BlockSpec((1, CONV_COLS), lambda i, j: (0, j)),
            par,
        ],
        out_specs=pl.BlockSpec((tm, CONV_COLS), lambda i, j: (i, j)),
        out_shape=jax.ShapeDtypeStruct((t, width), F32),
        scratch_shapes=[pltpu.VMEM((tm + 2 * CONV_HALO, CONV_COLS), F32)],
        compiler_params=_params(("parallel", "parallel")),
        name="conv_act",
    )(u_wide, u_wide, u_wide, w, conv_b.reshape(1, width), l2)


NARROW_GDN_A = 2 * SSD_HEADS
NARROW_GDN_B = NARROW_GDN_A + 2 * GDN_HEADS


def _mm(a, b):
    return jnp.dot(a.astype(BF16), b.astype(BF16), preferred_element_type=F32)


def _mm_nt(a, b):
    return lax.dot_general(a.astype(BF16), b.astype(BF16), (((1,), (1,)), ((), ())), preferred_element_type=F32)


def _mm_tn(a, b):
    return lax.dot_general(a.astype(BF16), b.astype(BF16), (((0,), (0,)), ((), ())), preferred_element_type=F32)


def _unit_triangular_inverses(lmats, row, col):
    q = lmats[0].shape[0]
    eye = (row == col).astype(F32)

    def same_block(shift):
        return (row >> shift) == (col >> shift)

    blk = same_block(3)
    ds = [jnp.where(blk, m, 0.0) for m in lmats]
    d2s = [_mm(d, d) for d in ds]
    d4s = [_mm(d2, d2) for d2 in d2s]
    xs = [_mm(eye - d, eye + d2) for d, d2 in zip(ds, d2s)]
    xs = [_mm(x, eye + d4) for x, d4 in zip(xs, d4s)]
    shift = 4
    while (1 << shift) <= q:
        level = same_block(shift) & jnp.logical_not(same_block(shift - 1))
        xes = [_mm(x, jnp.where(level, m, 0.0)) for x, m in zip(xs, lmats)]
        xs = [x - _mm(xe, x) for x, xe in zip(xs, xes)]
        shift += 1
    return xs


def _gdn_chunk_operands(q_ref, k_ref, v_ref, n_ref, ap_ref, bias_ref, direction, rows):
    cq = rows.stop - rows.start
    row = lax.broadcasted_iota(jnp.int32, (cq, cq), 0)
    col = lax.broadcasted_iota(jnp.int32, (cq, cq), 1)
    if direction == 0:
        incl, strict = col <= row, col < row
    else:
        incl, strict = col >= row, col > row
    nar = n_ref[rows, :]
    g_all = ap_ref[...] * jax.nn.softplus(nar + bias_ref[...])
    beta_all = jax.nn.sigmoid(nar)
    cum_op = jnp.concatenate([incl.astype(F32), jnp.ones((cq, cq), F32)], axis=0)
    cums = jnp.dot(cum_op, g_all, preferred_element_type=F32, precision=lax.Precision.HIGHEST)
    gc_all, tot_all = cums[:cq], cums[cq:]
    gc_t = gc_all.T
    egc_all = jnp.exp(gc_all)
    erem_all = jnp.exp(tot_all - gc_all)
    etot_all = jnp.exp(tot_all)
    heads = range(GDN_HEADS)
    la = [NARROW_GDN_A + direction * GDN_HEADS + h for h in heads]
    lb = [NARROW_GDN_B + direction * GDN_HEADS + h for h in heads]
    hs = [slice(h * GDN_HEAD_DIM, (h + 1) * GDN_HEAD_DIM) for h in heads]
    q = [q_ref[rows, hs[h]] for h in heads]
    k = [k_ref[rows, hs[h]] for h in heads]
    v = [v_ref[rows, hs[h]] for h in heads]
    beta = [beta_all[:, lb[h]:lb[h] + 1] for h in heads]
    egc = [egc_all[:, la[h]:la[h] + 1] for h in heads]
    decay = [jnp.exp(jnp.where(incl, gc_all[:, la[h]:la[h] + 1] - gc_t[la[h]:la[h] + 1, :], -1e30)) for h in heads]
    kb = [k[h] * beta[h] for h in heads]
    return dict(
        q=q, k=k, decay=decay, strict=[strict] * GDN_HEADS,
        qe=[q[h] * egc[h] for h in heads],
        rhs=[jnp.concatenate([v[h] * beta[h], kb[h] * egc[h]], axis=-1) for h in heads],
        kb=kb,
        ke=[k[h] * erem_all[:, la[h]:la[h] + 1] for h in heads],
        etot=[etot_all[0:1, la[h]:la[h] + 1] for h in heads],
    )


def _gdn_scan_kernel(qf, kf, vf, nf, qb, kb, vb, nb, ap_ref, bias_ref, of_ref, ob_ref, s_ref):
    @pl.when(pl.program_id(1) == 0)
    def _():
        s_ref[...] = jnp.zeros_like(s_ref)

    cq = GDN_CHUNK
    sub = qf.shape[0] // cq
    row = lax.broadcasted_iota(jnp.int32, (cq, cq), 0)
    col = lax.broadcasted_iota(jnp.int32, (cq, cq), 1)
    n = range(2 * GDN_HEADS)
    rows_f = [slice(j * cq, (j + 1) * cq) for j in range(sub)]
    rows_b = rows_f[::-1]
    chunks = []
    for j in range(sub):
        fwd = _gdn_chunk_operands(qf, kf, vf, nf, ap_ref, bias_ref, 0, rows_f[j])
        bwd = _gdn_chunk_operands(qb, kb, vb, nb, ap_ref, bias_ref, 1, rows_b[j])
        chunks.append({name: fwd[name] + bwd[name] for name in fwd})
    kk = [[_mm_nt(c['kb'][i], c['k'][i]) for i in n] for c in chunks]
    qk = [[_mm_nt(c['q'][i], c['k'][i]) for i in n] for c in chunks]
    lmats = [jnp.where(c['strict'][i], kk[j][i] * c['decay'][i], 0.0) for j, c in enumerate(chunks) for i in n]
    t_inv = _unit_triangular_inverses(lmats, row, col)
    s = [s_ref[i] for i in n]
    for j, c in enumerate(chunks):
        uw = [_mm(t_inv[j * len(n) + i], c['rhs'][i]) for i in n]
        ws = [_mm(uw[i][:, GDN_HEAD_DIM:], s[i]) for i in n]
        qs = [_mm(c['qe'][i], s[i]) for i in n]
        v_new = [uw[i][:, :GDN_HEAD_DIM] - ws[i] for i in n]
        outs = [qs[i] + _mm(qk[j][i] * c['decay'][i], v_new[i]) for i in n]
        upd = [_mm_tn(c['ke'][i], v_new[i]) for i in n]
        of_ref[rows_f[j], :] = jnp.concatenate(outs[:GDN_HEADS], axis=-1)
        ob_ref[rows_b[j], :] = jnp.concatenate(outs[GDN_HEADS:], axis=-1)
        s = [s[i] * c['etot'][i] + upd[i] for i in n]
    for i in n:
        s_ref[i] = s[i]


GDN_STEP_CHUNKS = 2


def gdn_scan(qkv, u_narrow, a_par, bias_par, group, carried, *, cq=GDN_STEP_CHUNKS * GDN_CHUNK):
    t = qkv.shape[0]
    w = GDN_WIDTH
    row0, batch, seq = group
    nc = seq // cq
    base = row0 // cq

    def fwd(col):
        return lambda b, c: (base + b * nc + c, col)

    def bwd(col):
        return lambda b, c: (base + b * nc + nc - 1 - c, col)

    wide = lambda idx: pl.BlockSpec((cq, w), idx)
    narrow = lambda idx: pl.BlockSpec((cq, LANES), idx)
    par = pl.BlockSpec((1, LANES), lambda b, c: (0, 0))
    return _group_call(
        _gdn_scan_kernel, "gdn_scan", (batch, nc),
        [wide(fwd(0)), wide(fwd(1)), wide(fwd(2)), narrow(fwd(0)),
         wide(bwd(0)), wide(bwd(1)), wide(bwd(2)), narrow(bwd(0)), par, par],
        (qkv, qkv, qkv, u_narrow, qkv, qkv, qkv, u_narrow, a_par, bias_par),
        [wide(fwd(0)), wide(bwd(0))],
        [jax.ShapeDtypeStruct((t, w), F32), jax.ShapeDtypeStruct((t, w), F32)],
        [pltpu.VMEM((2 * GDN_HEADS, GDN_HEAD_DIM, GDN_HEAD_DIM), F32)], carried)


def _gdn_gate_kernel(of_ref, ob_ref, z_ref, w_ref, o_ref):
    for h in range(GDN_HEADS):
        hs = slice(h * GDN_HEAD_DIM, (h + 1) * GDN_HEAD_DIM)
        o = _rms_normalize(of_ref[:, hs] + ob_ref[:, hs], w_ref[...])
        z = z_ref[:, hs]
        o_ref[:, hs] = o * (z * jax.nn.sigmoid(z))


def gdn_gate(o_f, o_b, u_wide, norm_w, *, tm=512):
    t, w = o_f.shape
    zblk = U_GDN_Z // GDN_WIDTH
    row = lambda i: (i, 0)
    return pl.pallas_call(
        _gdn_gate_kernel,
        grid=(t // tm,),
        in_specs=[pl.BlockSpec((tm, w), row), pl.BlockSpec((tm, w), row), pl.BlockSpec((tm, w), lambda i: (i, zblk)),
                  pl.BlockSpec((1, GDN_HEAD_DIM), lambda i: (0, 0))],
        out_specs=pl.BlockSpec((tm, w), row),
        out_shape=jax.ShapeDtypeStruct((t, w), F32),
        compiler_params=_params(("parallel",)),
        name="gdn_gate",
    )(o_f, o_b, u_wide, norm_w.reshape(1, -1))


def gdn_mixer(u_wide, u_narrow, groups, conv_w, dt_bias, a_log, norm_w):
    ones, zeros = jnp.ones((GDN_WIDTH,), F32), jnp.zeros((GDN_WIDTH,), F32)
    l2 = jnp.stack([jnp.concatenate([ones, ones, zeros]),
                    jnp.concatenate([ones * GDN_HEAD_DIM ** -0.5, ones, zeros])])
    qkv = conv_act(u_wide, U_GDN_QKV, GDN_CONV_DIM, conv_w, jnp.zeros((GDN_CONV_DIM,), F32), groups,
                   l2=l2, head_dim=GDN_HEAD_DIM)
    lanes = slice(NARROW_GDN_A, NARROW_GDN_B)
    a_par = jnp.zeros((1, LANES), F32).at[0, lanes].set(-jnp.exp(a_log).reshape(-1))
    bias_par = jnp.zeros((1, LANES), F32).at[0, lanes].set(dt_bias.reshape(-1))
    outs = None
    for group in groups:
        outs = gdn_scan(qkv, u_narrow, a_par, bias_par, group, outs)
    return gdn_gate(outs[0], outs[1], u_wide, norm_w)


SSD_BC = SSD_GROUPS * SSD_STATE
SSD_REP = SSD_HEADS // SSD_GROUPS


def _ssd_chunk_operands(xbc_ref, n_ref, ap_ref, bias_ref, direction, rows):
    cq = rows.stop - rows.start
    row = lax.broadcasted_iota(jnp.int32, (cq, cq), 0)
    col = lax.broadcasted_iota(jnp.int32, (cq, cq), 1)
    incl = col <= row if direction == 0 else col >= row
    dt_all = jax.nn.softplus(n_ref[rows, :] + bias_ref[...])
    cum_op = jnp.concatenate([incl.astype(F32), jnp.ones((cq, cq), F32)], axis=0)
    cums = jnp.dot(cum_op, dt_all * ap_ref[...], preferred_element_type=F32, precision=lax.Precision.HIGHEST)
    ac_all, tot_all = cums[:cq], cums[cq:]
    ac_t = ac_all.T
    eac_all = jnp.exp(ac_all)
    erem_all = jnp.exp(tot_all - ac_all)
    etot_all = jnp.exp(tot_all)
    groups = range(SSD_GROUPS)
    bm = [xbc_ref[rows, SSD_WIDTH + g * SSD_STATE:SSD_WIDTH + (g + 1) * SSD_STATE] for g in groups]
    cm = [xbc_ref[rows, SSD_WIDTH + SSD_BC + g * SSD_STATE:SSD_WIDTH + SSD_BC + (g + 1) * SSD_STATE] for g in groups]
    cb = [_mm_nt(cm[g], bm[g]) for g in groups]
    heads = range(SSD_HEADS)
    ln = [direction * SSD_HEADS + h for h in heads]
    seg = [jnp.exp(jnp.where(incl, ac_all[:, ln[h]:ln[h] + 1] - ac_t[ln[h]:ln[h] + 1, :], -1e30)) for h in heads]
    return dict(
        scores=[cb[h // SSD_REP] * seg[h] for h in heads],
        xdt=[xbc_ref[rows, h * SSD_HEAD_DIM:(h + 1) * SSD_HEAD_DIM] * dt_all[:, ln[h]:ln[h] + 1] for h in heads],
        b_end=[bm[h // SSD_REP] * erem_all[:, ln[h]:ln[h] + 1] for h in heads],
        c_in=[cm[h // SSD_REP] * eac_all[:, ln[h]:ln[h] + 1] for h in heads],
        etot=[etot_all[0:1, ln[h]:ln[h] + 1] for h in heads],
    )


def _ssd_scan_kernel(xf, nf, xb, nb, ap_ref, bias_ref, of_ref, ob_ref, s_ref):
    @pl.when(pl.program_id(1) == 0)
    def _():
        s_ref[...] = jnp.zeros_like(s_ref)

    cq = SSD_CHUNK
    sub = xf.shape[0] // cq
    n = range(2 * SSD_HEADS)
    rows_f = [slice(j * cq, (j + 1) * cq) for j in range(sub)]
    rows_b = rows_f[::-1]
    chunks = []
    for j in range(sub):
        fwd = _ssd_chunk_operands(xf, nf, ap_ref, bias_ref, 0, rows_f[j])
        bwd = _ssd_chunk_operands(xb, nb, ap_ref, bias_ref, 1, rows_b[j])
        chunks.append({name: fwd[name] + bwd[name] for name in fwd})
    y_diag = [[_mm(c['scores'][i], c['xdt'][i]) for i in n] for c in chunks]
    upd = [[_mm_tn(c['b_end'][i], c['xdt'][i]) for i in n] for c in chunks]
    s = [s_ref[i] for i in n]
    for j, c in enumerate(chunks):
        y = [y_diag[j][i] + _mm(c['c_in'][i], s[i]) for i in n]
        of_ref[rows_f[j], :] = jnp.concatenate(y[:SSD_HEADS], axis=-1)
        ob_ref[rows_b[j], :] = jnp.concatenate(y[SSD_HEADS:], axis=-1)
        s = [s[i] * c['etot'][i] + upd[j][i] for i in n]
    for i in n:
        s_ref[i] = s[i]


SSD_STEP_CHUNKS = 2


def ssd_scan(xbc, u_narrow, a_par, bias_par, group, carried, *, cq=SSD_STEP_CHUNKS * SSD_CHUNK):
    t = xbc.shape[0]
    row0, batch, seq = group
    nc = seq // cq
    base = row0 // cq

    def fwd(b, c):
        return (base + b * nc + c, 0)

    def bwd(b, c):
        return (base + b * nc + nc - 1 - c, 0)

    wide = lambda idx: pl.BlockSpec((cq, SSD_CONV_DIM), idx)
    narrow = lambda idx: pl.BlockSpec((cq, LANES), idx)
    out = lambda idx: pl.BlockSpec((cq, SSD_WIDTH), idx)
    par = pl.BlockSpec((1, LANES), lambda b, c: (0, 0))
    return _group_call(
        _ssd_scan_kernel, "ssd_scan", (batch, nc),
        [wide(fwd), narrow(fwd), wide(bwd), narrow(bwd), par, par],
        (xbc, u_narrow, xbc, u_narrow, a_par, bias_par),
        [out(fwd), out(bwd)],
        [jax.ShapeDtypeStruct((t, SSD_WIDTH), F32), jax.ShapeDtypeStruct((t, SSD_WIDTH), F32)],
        [pltpu.VMEM((2 * SSD_HEADS, SSD_STATE, SSD_HEAD_DIM), F32)], carried)


def _ssd_gate_kernel(yf_ref, yb_ref, x_ref, z_ref, d_ref, w_ref, o_ref):
    z = z_ref[...]
    y = (yf_ref[...] + yb_ref[...] + x_ref[...] * d_ref[...]) * (z * jax.nn.sigmoid(z))
    o_ref[...] = _rms_normalize(y, w_ref[...])


def ssd_gate(y_f, y_b, xbc, u_wide, d_wide, norm_w, *, tm=512):
    t, w = y_f.shape
    row = lambda i: (i, 0)
    vec = pl.BlockSpec((1, w), lambda i: (0, 0))
    return pl.pallas_call(
        _ssd_gate_kernel,
        grid=(t // tm,),
        in_specs=[pl.BlockSpec((tm, w), row), pl.BlockSpec((tm, w), row), pl.BlockSpec((tm, w), row),
                  pl.BlockSpec((tm, w), lambda i: (i, U_SSD_Z // SSD_WIDTH)), vec, vec],
        out_specs=pl.BlockSpec((tm, w), row),
        out_shape=jax.ShapeDtypeStruct((t, w), F32),
        compiler_params=_params(("parallel",)),
        name="ssd_gate",
    )(y_f, y_b, xbc, u_wide, d_wide, norm_w.reshape(1, -1))


def ssd_mixer(u_wide, u_narrow, groups, conv_w, conv_b, dt_bias, a_log, d_skip, norm_w):
    xbc = conv_act(u_wide, U_SSD_XBC, SSD_CONV_DIM, conv_w, conv_b, groups)
    lanes = slice(0, 2 * SSD_HEADS)
    a_par = jnp.zeros((1, LANES), F32).at[0, lanes].set(-jnp.exp(a_log).reshape(-1))
    bias_par = jnp.zeros((1, LANES), F32).at[0, lanes].set(dt_bias.reshape(-1))
    outs = None
    for group in groups:
        outs = ssd_scan(xbc, u_narrow, a_par, bias_par, group, outs)
    d_wide = jnp.repeat(d_skip, SSD_HEAD_DIM).reshape(1, SSD_WIDTH)
    return ssd_gate(outs[0], outs[1], xbc, u_wide, d_wide, norm_w)


FNET_N2 = 128
FNET_PAIR = 2 * FNET_HEAD_DIM
FNET_SLAB = 16


def _dft_cos_sin(n, scale):
    idx = jnp.arange(n, dtype=jnp.int32)
    ang = (2.0 * math.pi / n) * ((idx[:, None] * idx[None, :]) % n).astype(F32)
    return jnp.cos(ang) * scale, jnp.sin(ang) * scale


def _fnet_channel_kernel(u_ref, m_ref, y_ref):
    for h in range(FNET_HEADS):
        uh = u_ref[:, h * FNET_HEAD_DIM:(h + 1) * FNET_HEAD_DIM].astype(BF16)
        y_ref[h] = jnp.dot(uh, m_ref[...], preferred_element_type=F32).astype(BF16)


def _re_im_products(c_ref, s_ref, x):
    p = jnp.dot(c_ref[...], x, preferred_element_type=F32)
    q = jnp.dot(s_ref[...], x, preferred_element_type=F32)
    res = []
    for j in range(x.shape[1] // FNET_PAIR):
        re = slice(j * FNET_PAIR, j * FNET_PAIR + FNET_HEAD_DIM)
        im = slice(j * FNET_PAIR + FNET_HEAD_DIM, (j + 1) * FNET_PAIR)
        res.append((p[:, re] + q[:, im], p[:, im] - q[:, re]))
    return res


def _fnet_stage1_kernel(x_ref, c_ref, s_ref, tc_ref, ts_ref, o_ref):
    for j, (ar, ai) in enumerate(_re_im_products(c_ref, s_ref, x_ref[0])):
        lanes = slice(j * FNET_HEAD_DIM, (j + 1) * FNET_HEAD_DIM)
        tc, ts = tc_ref[:, lanes], ts_ref[:, lanes]
        o_ref[0, j] = jnp.concatenate([ar * tc + ai * ts, ai * tc - ar * ts], axis=-1).astype(BF16)


def _fnet_stage2_kernel(x_ref, c_ref, s_ref, o_ref):
    o_ref[0] = jnp.concatenate([re for re, _ in _re_im_products(c_ref, s_ref, x_ref[0])], axis=-1)


def _fnet_out_kernel(f_ref, w_ref, b_ref, o_ref):
    acc = b_ref[...]
    for h in range(FNET_HEADS):
        acc = acc + jnp.dot(f_ref[h].astype(BF16), w_ref[h * FNET_HEAD_DIM:(h + 1) * FNET_HEAD_DIM, :],
                            preferred_element_type=F32)
    o_ref[...] = acc


def _fnet_sequence_dft(y, batch, seq, *, slab=FNET_SLAB):
    t = y.shape[1]
    n1, n2 = seq // FNET_N2, FNET_N2
    hb = FNET_HEADS * batch
    c1, s1 = (m.astype(BF16) for m in _dft_cos_sin(n1, n1 ** -0.5))
    k1 = jnp.arange(n1, dtype=jnp.int32)[:, None]
    m2 = jnp.arange(n2, dtype=jnp.int32)[None, :]
    ang = (2.0 * math.pi / seq) * ((k1 * m2) % seq).astype(F32)
    tw_c = jnp.repeat(jnp.cos(ang), FNET_HEAD_DIM, axis=1)
    tw_s = jnp.repeat(jnp.sin(ang), FNET_HEAD_DIM, axis=1)
    mat1 = pl.BlockSpec((n1, n1), lambda g, j: (0, 0))
    a = pl.pallas_call(
        _fnet_stage1_kernel,
        grid=(hb, n2 // slab),
        in_specs=[pl.BlockSpec((1, n1, slab * FNET_PAIR), lambda g, j: (g, 0, j)), mat1, mat1,
                  pl.BlockSpec((n1, slab * FNET_HEAD_DIM), lambda g, j: (0, j)),
                  pl.BlockSpec((n1, slab * FNET_HEAD_DIM), lambda g, j: (0, j))],
        out_specs=pl.BlockSpec((1, slab, n1, FNET_PAIR), lambda g, j: (g, j, 0, 0)),
        out_shape=jax.ShapeDtypeStruct((hb, n2, n1, FNET_PAIR), BF16),
        compiler_params=_params(("parallel", "parallel")),
        name="fnet_stage1",
    )(y.reshape(hb, n1, n2 * FNET_PAIR), c1, s1, tw_c, tw_s)

    c2, s2 = (m.astype(BF16) for m in _dft_cos_sin(n2, n2 ** -0.5))
    kslab = min(slab, n1)
    mat2 = pl.BlockSpec((n2, n2), lambda g, j: (0, 0))
    f = pl.pallas_call(
        _fnet_stage2_kernel,
        grid=(hb, n1 // kslab),
        in_specs=[pl.BlockSpec((1, n2, kslab * FNET_PAIR), lambda g, j: (g, 0, j)), mat2, mat2],
        out_specs=pl.BlockSpec((1, n2, kslab * FNET_HEAD_DIM), lambda g, j: (g, 0, j)),
        out_shape=jax.ShapeDtypeStruct((hb, n2, n1 * FNET_HEAD_DIM), F32),
        compiler_params=_params(("parallel", "parallel")),
        name="fnet_stage2",
    )(a.reshape(hb, n2, n1 * FNET_PAIR), c2, s2)
    return f.reshape(FNET_HEADS, t, FNET_HEAD_DIM)


def fourier_mixer(u_wide, groups, w, b, *, tm=512):
    t = u_wide.shape[0]
    cc, sc = _dft_cos_sin(FNET_HEAD_DIM, FNET_HEAD_DIM ** -0.5)
    y = pl.pallas_call(
        _fnet_channel_kernel,
        grid=(t // tm,),
        in_specs=[pl.BlockSpec((tm, FNET_WIDTH), lambda i: (i, U_FNET // FNET_WIDTH)),
                  pl.BlockSpec((FNET_HEAD_DIM, FNET_PAIR), lambda i: (0, 0))],
        out_specs=pl.BlockSpec((FNET_HEADS, tm, FNET_PAIR), lambda i: (0, i, 0)),
        out_shape=jax.ShapeDtypeStruct((FNET_HEADS, t, FNET_PAIR), BF16),
        compiler_params=_params(("parallel",)),
        name="fnet_channel",
    )(u_wide, jnp.concatenate([cc, -sc], axis=-1).astype(BF16))
    f = jnp.concatenate([_fnet_sequence_dft(y[:, row0:row0 + batch * seq], batch, seq) for row0, batch, seq in groups],
                        axis=1)
    return pl.pallas_call(
        _fnet_out_kernel,
        grid=(t // tm,),
        in_specs=[pl.BlockSpec((FNET_HEADS, tm, FNET_HEAD_DIM), lambda i: (0, i, 0)),
                  pl.BlockSpec((FNET_WIDTH, FNET_WIDTH), lambda i: (0, 0)),
                  pl.BlockSpec((1, FNET_WIDTH), lambda i: (0, 0))],
        out_specs=pl.BlockSpec((tm, FNET_WIDTH), lambda i: (i, 0)),
        out_shape=jax.ShapeDtypeStruct((t, FNET_WIDTH), F32),
        compiler_params=_params(("parallel",)),
        name="fnet_out",
    )(f, w.astype(BF16), b.reshape(1, -1))


S5_TILE = 256
S5_NSTATE = S5_GROUPS * S5_STATE
S5_COLS = 512
S5_BLOCKS = S5_WIDTH // LANES


def _s5_scan_kernel(u_ref, pi_ref, pit_ref, b_ref, c_ref, lre_ref, lim_ref, o_ref,
                    sre, sim, pre, pim, car_re, car_im, e_re, e_im, cin_re, cin_im):
    tl = u_ref.shape[0]
    steps = tl // 8
    first_tile = pl.program_id(2) == 0

    def col(j):
        return slice(j * S5_COLS, (j + 1) * S5_COLS)

    def rows(t):
        return pl.ds(pl.multiple_of(t * 8, 8), 8)

    def bcast(ref, j):
        return jnp.broadcast_to(ref[0, :, col(j)], (8, S5_COLS))

    @pl.when(first_tile)
    def _():
        car_re[...] = jnp.zeros_like(car_re)
        car_im[...] = jnp.zeros_like(car_im)
        for j in range(S5_BLOCKS):
            lr, li = bcast(lre_ref, j), bcast(lim_ref, j)

            def power_step(t, p, j=j, lr=lr, li=li):
                pr, pi_ = p
                pre[rows(t), col(j)] = pr
                pim[rows(t), col(j)] = pi_
                return pr * lr - pi_ * li, pr * li + pi_ * lr

            lax.fori_loop(0, steps, power_step, (lr, li))

    up = jnp.dot(pi_ref[0], u_ref[...].astype(BF16), preferred_element_type=F32).astype(BF16)
    for j in range(S5_BLOCKS):
        uj = up[:, j * LANES:(j + 1) * LANES]
        sre[:, col(j)] = jnp.dot(uj, b_ref[0, j, 0], preferred_element_type=F32)
        sim[:, col(j)] = jnp.dot(uj, b_ref[0, j, 1], preferred_element_type=F32)

    for j in range(S5_BLOCKS):
        lr, li = bcast(lre_ref, j), bcast(lim_ref, j)

        def scan_step(t, s, j=j, lr=lr, li=li):
            sr, si = s
            nr = lr * sr - li * si + sre[rows(t), col(j)]
            ni = lr * si + li * sr + sim[rows(t), col(j)]
            sre[rows(t), col(j)] = nr
            sim[rows(t), col(j)] = ni
            return nr, ni

        zero = jnp.zeros((8, S5_COLS), F32)
        er, ei = lax.fori_loop(0, steps, scan_step, (zero, zero), unroll=4)
        e_re[:, col(j)] = er
        e_im[:, col(j)] = ei

    last = pl.ds((steps - 1) * 8, 1)
    lsr, lsi = pre[last, :], pim[last, :]
    cr, ci = car_re[...], car_im[...]
    for r in range(8):
        cin_re[r:r + 1, :] = cr
        cin_im[r:r + 1, :] = ci
        cr, ci = (e_re[r:r + 1, :] + lsr * cr - lsi * ci, e_im[r:r + 1, :] + lsr * ci + lsi * cr)
    car_re[...] = cr
    car_im[...] = ci

    for j in range(S5_BLOCKS):
        cr, ci = cin_re[:, col(j)], cin_im[:, col(j)]

        def fix_step(t, carry, j=j, cr=cr, ci=ci):
            pr, pi_ = pre[rows(t), col(j)], pim[rows(t), col(j)]
            sre[rows(t), col(j)] = sre[rows(t), col(j)] + pr * cr - pi_ * ci
            sim[rows(t), col(j)] = sim[rows(t), col(j)] + pr * ci + pi_ * cr
            return carry

        lax.fori_loop(0, steps, fix_step, 0, unroll=4)

    ys = []
    for j in range(S5_BLOCKS):
        ys.append(jnp.dot(sre[:, col(j)].astype(BF16), c_ref[0, j, 0], preferred_element_type=F32)
                  + jnp.dot(sim[:, col(j)].astype(BF16), c_ref[0, j, 1], preferred_element_type=F32))
    y = jnp.concatenate(ys, axis=-1)
    y_hi = y.astype(BF16)
    y_lo = (y - y_hi.astype(F32)).astype(BF16)
    pit = pit_ref[0]
    o_ref[0] = jnp.dot(pit, y_hi, preferred_element_type=F32) + jnp.dot(pit, y_lo, preferred_element_type=F32)


def _s5_operators(lam_re, lam_im, log_step, b_re, b_im, c_re, c_im, tl):
    lam = lax.complex(lam_re, lam_im)
    delta = jnp.exp(log_step)[..., None]
    lam_bar = jnp.exp(lam * delta)
    b_scale = (lam_bar - 1.0) / lam
    gpb = S5_GROUPS // S5_BLOCKS
    same_group = (jnp.arange(LANES)[:, None] // S5_GROUP_CH) == (jnp.arange(S5_COLS)[None, :] // S5_STATE)
    bd = lax.complex(b_re, b_im)[None] * b_scale[..., None]
    bd = jnp.swapaxes(bd.reshape(2, S5_BLOCKS, gpb, S5_STATE, S5_GROUP_CH), 3, 4).reshape(2, S5_BLOCKS, LANES, S5_STATE)
    bd = jnp.tile(bd, (1, 1, 1, gpb))
    bmat = jnp.stack([jnp.where(same_group, part, 0.0) for part in (bd.real, bd.imag)], axis=2).astype(BF16)
    cd = lax.complex(c_re, c_im).reshape(2, S5_BLOCKS, gpb, S5_GROUP_CH, S5_STATE)
    cd = jnp.transpose(cd, (0, 1, 4, 2, 3)).reshape(2, S5_BLOCKS, S5_STATE, LANES)
    cd = jnp.tile(cd, (1, 1, gpb, 1))
    cmat = jnp.stack([jnp.where(same_group.T, part, 0.0) for part in (cd.real, -cd.imag)], axis=2).astype(BF16)
    lre = lam_bar.real.reshape(2, 1, S5_NSTATE)
    lim = lam_bar.imag.reshape(2, 1, S5_NSTATE)
    steps = tl // 8
    rt = jnp.arange(tl)
    src = (rt % 8) * steps + rt // 8
    fwd = (src[:, None] == rt[None, :])
    bwd = ((tl - 1 - src)[:, None] == rt[None, :])
    pi = jnp.stack([fwd, bwd]).astype(BF16)
    return pi, jnp.swapaxes(pi, 1, 2), bmat, cmat, lre, lim


def s5_scan(u_wide, ops, group, carried, *, tl=S5_TILE):
    pi, pit, bmat, cmat, lre, lim = ops
    t = u_wide.shape[0]
    row0, batch, seq = group
    nt = seq // tl
    base = row0 // tl
    ublk = U_S5 // S5_WIDTH

    def tile_idx(b, d, i):
        return base + b * nt + jnp.where(d == 0, i, nt - 1 - i)

    state = pltpu.VMEM((tl, S5_NSTATE), F32)
    row = pltpu.VMEM((1, S5_NSTATE), F32)
    seg = pltpu.VMEM((8, S5_NSTATE), F32)
    return _group_call(
        _s5_scan_kernel, "s5_scan", (batch, 2, nt),
        [
            pl.BlockSpec((tl, S5_WIDTH), lambda b, d, i: (tile_idx(b, d, i), ublk)),
            pl.BlockSpec((1, tl, tl), lambda b, d, i: (d, 0, 0)),
            pl.BlockSpec((1, tl, tl), lambda b, d, i: (d, 0, 0)),
            pl.BlockSpec((1, S5_BLOCKS, 2, LANES, S5_COLS), lambda b, d, i: (d, 0, 0, 0, 0)),
            pl.BlockSpec((1, S5_BLOCKS, 2, S5_COLS, LANES), lambda b, d, i: (d, 0, 0, 0, 0)),
            pl.BlockSpec((1, 1, S5_NSTATE), lambda b, d, i: (d, 0, 0)),
            pl.BlockSpec((1, 1, S5_NSTATE), lambda b, d, i: (d, 0, 0)),
        ],
        (u_wide, pi, pit, bmat, cmat, lre, lim),
        [pl.BlockSpec((1, tl, S5_WIDTH), lambda b, d, i: (d, tile_idx(b, d, i), 0))],
        [jax.ShapeDtypeStruct((2, t, S5_WIDTH), F32)],
        [state, state, state, state, row, row, seg, seg, seg, seg], carried)[0]


def _s5_gate_kernel(y_ref, u_ref, d_ref, w_ref, b_ref, o_ref):
    y = y_ref[0] + y_ref[1] + u_ref[...] * d_ref[...]
    h = jax.nn.gelu(y)
    gate = jnp.dot(h.astype(BF16), w_ref[...], preferred_element_type=F32) + b_ref[...]
    o_ref[...] = h * jax.nn.sigmoid(gate)


def s5_gate(y2, u_wide, d_skip, glu_w, glu_b, *, tm=512):
    t = u_wide.shape[0]
    ublk = U_S5 // S5_WIDTH
    return pl.pallas_call(
        _s5_gate_kernel,
        grid=(t // tm,),
        in_specs=[
            pl.BlockSpec((2, tm, S5_WIDTH), lambda i: (0, i, 0)),
            pl.BlockSpec((tm, S5_WIDTH), lambda i: (i, ublk)),
            pl.BlockSpec((1, S5_WIDTH), lambda i: (0, 0)),
            pl.BlockSpec((S5_WIDTH, S5_WIDTH), lambda i: (0, 0)),
            pl.BlockSpec((1, S5_WIDTH), lambda i: (0, 0)),
        ],
        out_specs=pl.BlockSpec((tm, S5_WIDTH), lambda i: (i, 0)),
        out_shape=jax.ShapeDtypeStruct((t, S5_WIDTH), F32),
        compiler_params=_params(("parallel",)),
        name="s5_gate",
    )(y2, u_wide, d_skip.reshape(1, -1), glu_w.astype(BF16), glu_b.reshape(1, -1))


def s5_mixer(u_wide, groups, ops, d_skip, glu_w, glu_b):
    y2 = None
    for group in groups:
        y2 = s5_scan(u_wide, ops, group, None if y2 is None else (y2,))
    return s5_gate(y2, u_wide, d_skip, glu_w, glu_b)


def mixers(u_wide, u_narrow, groups, p, i):
    y_ssd = ssd_mixer(u_wide, u_narrow, groups, p['ssd_conv_w'][i], p['ssd_conv_b'][i], p['ssd_dt_bias'][i],
                      p['ssd_a_log'][i], p['ssd_d'][i], p['ssd_norm'][i])
    y_gdn = gdn_mixer(u_wide, u_narrow, groups, p['gdn_conv_w'][i], p['gdn_dt_bias'][i], p['gdn_a_log'][i],
                      p['gdn_norm'][i])
    y_fnet = fourier_mixer(u_wide, groups, p['fnet_w'][i], p['fnet_b'][i])
    s5_ops = _s5_operators(p['s5_lambda_re'][i], p['s5_lambda_im'][i], p['s5_log_step'][i], p['s5_b_re'][i],
                           p['s5_b_im'][i], p['s5_c_re'][i], p['s5_c_im'][i], S5_TILE)
    y_s5 = s5_mixer(u_wide, groups, s5_ops, p['s5_d'][i], p['s5_glu_w'][i], p['s5_glu_b'][i])
    return y_ssd, y_gdn, y_fnet, y_s5


def _split_w_in(w_in):
    o1 = SSD_IN
    o2 = o1 + GDN_IN
    ssd_z = w_in[..., :SSD_WIDTH]
    ssd_xbc = w_in[..., SSD_WIDTH:SSD_WIDTH + SSD_CONV_DIM]
    ssd_dt = w_in[..., SSD_WIDTH + SSD_CONV_DIM:o1]
    gdn_qkv = w_in[..., o1:o1 + GDN_CONV_DIM]
    gdn_z = w_in[..., o1 + GDN_CONV_DIM:o1 + 4 * GDN_WIDTH]
    gdn_ab = w_in[..., o1 + 4 * GDN_WIDTH:o2]
    rest = w_in[..., o2:]
    wide = jnp.concatenate([ssd_z, ssd_xbc, gdn_qkv, gdn_z, rest], axis=-1).astype(BF16)
    pad = jnp.zeros(w_in.shape[:-1] + (LANES - NARROW,), w_in.dtype)
    narrow = jnp.concatenate([ssd_dt, gdn_ab, pad], axis=-1).astype(BF16)
    return wide, narrow


def kernel(x_prompt, x_sample, norm_mix, w_in, ssd_conv_w, ssd_conv_b, ssd_dt_bias, ssd_a_log, ssd_d, ssd_norm,
           gdn_conv_w, gdn_dt_bias, gdn_a_log, gdn_norm, fnet_w, fnet_b, s5_lambda_re, s5_lambda_im, s5_log_step,
           s5_b_re, s5_b_im, s5_c_re, s5_c_im, s5_d, s5_glu_w, s5_glu_b, w_out, norm_ffn, ffn_w_gate, ffn_w_up,
           ffn_w_down, router_w, moe_w_gate, moe_w_up, moe_w_down, norm_final):
    p = dict(ssd_conv_w=ssd_conv_w, ssd_conv_b=ssd_conv_b, ssd_dt_bias=ssd_dt_bias, ssd_a_log=ssd_a_log, ssd_d=ssd_d,
             ssd_norm=ssd_norm, gdn_conv_w=gdn_conv_w, gdn_dt_bias=gdn_dt_bias, gdn_a_log=gdn_a_log,
             gdn_norm=gdn_norm, fnet_w=fnet_w, fnet_b=fnet_b, s5_lambda_re=s5_lambda_re, s5_lambda_im=s5_lambda_im,
             s5_log_step=s5_log_step, s5_b_re=s5_b_re, s5_b_im=s5_b_im, s5_c_re=s5_c_re, s5_c_im=s5_c_im, s5_d=s5_d,
             s5_glu_w=s5_glu_w, s5_glu_b=s5_glu_b)
    bp, lp, d = x_prompt.shape
    bs, ls, _ = x_sample.shape
    tp = bp * lp
    x = jnp.concatenate([x_prompt.reshape(tp, d), x_sample.reshape(bs * ls, d)], axis=0)
    groups = ((0, bp, lp), (tp, bs, ls))
    w_wide, w_narrow = _split_w_in(w_in)
    w_out_b = w_out.astype(BF16)
    ffn_w_gate, ffn_w_up, ffn_w_down = (w.astype(BF16) for w in (ffn_w_gate, ffn_w_up, ffn_w_down))
    depth = w_in.shape[0]
    for i in range(depth):
        u_wide, u_narrow = norm_inproj(x, norm_mix[i], w_wide[i], w_narrow[i])
        x = outproj_residual(mixers(u_wide, u_narrow, groups, p, i), w_out_b[i], x)
        j = i // 2
        if i % 2 == 0:
            x = dense_ffn(x, norm_ffn[i], ffn_w_gate, ffn_w_up, ffn_w_down, j)
        else:
            x = moe_block(x, norm_ffn[i], router_w[j], moe_w_gate, moe_w_up, moe_w_down, j)
    y = final_norm(x, norm_final)
    return y[:tp].reshape(bp, lp, d), y[tp:].reshape(bs, ls, d)
```
